```python
import jax, jax.numpy as jnp
from jax import lax
import numpy as np

D_MODEL = 2048
BATCH = 16
SEQ = 2048
DEPTH = 1
DEC_BATCH = 4
DEC_SEQ = 8192
PAST_LEN = 128

HEAD_DIM = 128
ATTN_GROUPS = ((128, 1), (512, 4), (2048, 16))
HEADS_PER_GROUP = 4
N_ATTN_HEADS = HEADS_PER_GROUP * len(ATTN_GROUPS)
ATTN_WIDTH = N_ATTN_HEADS * HEAD_DIM
ATTN_OUT_WIDTH = HEADS_PER_GROUP * HEAD_DIM
ROPE_THETA = 10000.0

SSD_EXPAND = 2
SSD_INNER = SSD_EXPAND * D_MODEL
SSD_HEAD_DIM = 64
SSD_HEADS = SSD_INNER // SSD_HEAD_DIM
SSD_GROUPS = 8
SSD_STATE = 128
SSD_CONV = 5
SSD_CHUNK = 128
SSD_CONV_DIM = SSD_INNER + 2 * SSD_GROUPS * SSD_STATE

D_FF = 5632
FFN_CONV = 3

EPS = 1e-6
NEG_INF = -1e30

IN_WIDTHS = (ATTN_WIDTH, ATTN_WIDTH, ATTN_WIDTH, SSD_INNER, SSD_CONV_DIM, 2 * SSD_HEADS, D_MODEL, D_MODEL)
IN_WIDTH = int(sum(IN_WIDTHS))
IN_OFFSETS = [int(o) for o in np.cumsum(IN_WIDTHS)[:-1]]

kernel_name = "hybrid_dilated_attn_ssd_encoder"


def rmsnorm(x, w):
    x32 = x.astype(jnp.float32)
    return x32 * lax.rsqrt(jnp.mean(x32 * x32, axis=-1, keepdims=True) + EPS) * w.astype(jnp.float32)


def group_rmsnorm(y, w, groups):
    b, s, c = y.shape
    yg = y.reshape(b, s, groups, c // groups)
    yg = yg * lax.rsqrt(jnp.mean(yg * yg, axis=-1, keepdims=True) + EPS)
    return yg.reshape(b, s, c) * w.astype(jnp.float32)


def depthwise_conv_centred(x, w, b):
    k_w = w.shape[0]
    s = x.shape[1]
    pad = k_w // 2
    xp = jnp.pad(x, ((0, 0), (pad, pad), (0, 0)))
    out = xp[:, 0:s] * w[0]
    for j in range(1, k_w):
        out = out + xp[:, j:j + s] * w[j]
    return out + b


def rotary_tables(s):
    pos = jnp.arange(s, dtype=jnp.float32)
    inv_freq = ROPE_THETA ** (-jnp.arange(0, HEAD_DIM, 2, dtype=jnp.float32) / HEAD_DIM)
    ang = pos[:, None] * inv_freq[None, :]
    return jnp.cos(ang)[:, None, :], jnp.sin(ang)[:, None, :]


def apply_rotary(t, cos, sin):
    half = t.shape[-1] // 2
    t1, t2 = t[..., :half].astype(jnp.float32), t[..., half:].astype(jnp.float32)
    return jnp.concatenate([t1 * cos - t2 * sin, t1 * sin + t2 * cos], axis=-1)


def dilated_window_attention(q, k, v, dilation, radius):
    bsz, s, h, e = q.shape
    n = s // dilation
    blk = radius
    nb = -(-n // blk)
    n_pad = nb * blk

    def sub(t):
        return t.reshape(bsz, n, dilation, h, e).transpose(0, 2, 1, 3, 4)

    qb = jnp.pad(sub(q), ((0, 0), (0, 0), (0, n_pad - n), (0, 0), (0, 0))).reshape(bsz, dilation, nb, blk, h, e)

    def key_blocks(t):
        tp = jnp.pad(sub(t), ((0, 0), (0, 0), (blk, n_pad - n + blk), (0, 0), (0, 0)))
        tp = tp.reshape(bsz, dilation, nb + 2, blk, h, e)
        return jnp.concatenate([tp[:, :, :-2], tp[:, :, 1:-1], tp[:, :, 2:]], axis=3)

    kb = key_blocks(k)
    vb = key_blocks(v).astype(jnp.float32)
    q_idx = jnp.arange(nb)[:, None] * blk + jnp.arange(blk)[None, :]
    k_idx = jnp.arange(nb)[:, None] * blk - blk + jnp.arange(3 * blk)[None, :]
    rel = k_idx[:, None, :] - q_idx[:, :, None]
    valid = (jnp.abs(rel) <= radius) & (k_idx[:, None, :] >= 0) & (k_idx[:, None, :] < n)
    scores = jnp.einsum('bdnqhe,bdnkhe->bdnhqk', qb, kb).astype(jnp.float32) * (e ** -0.5)
    scores = jnp.where(valid[None, None, :, None], scores, NEG_INF)
    m = jnp.max(scores, axis=-1, keepdims=True)
    p = jnp.exp(scores - m)
    den = jnp.sum(p, axis=-1, keepdims=True)
    out = jnp.einsum('bdnhqk,bdnkhe->bdnqhe', p / den, vb)
    lse = (m + jnp.log(den))[..., 0].transpose(0, 1, 2, 4, 3)
    out = out.reshape(bsz, dilation, n_pad, h, e)[:, :, :n].transpose(0, 2, 1, 3, 4).reshape(bsz, s, h, e)
    lse = lse.reshape(bsz, dilation, n_pad, h)[:, :, :n].transpose(0, 2, 1, 3).reshape(bsz, s, h)
    return out, lse


def ssd_chunked_scan(x, dt, a, bm, cm):
    bsz, s, nh, hp = x.shape
    g, ns = bm.shape[2], bm.shape[3]
    r = nh // g
    L = SSD_CHUNK
    nc = s // L
    xs = x.reshape(bsz, nc, L, g, r, hp).transpose(1, 0, 2, 3, 4, 5)
    dts = dt.reshape(bsz, nc, L, g, r).transpose(1, 0, 2, 3, 4)
    bs = bm.reshape(bsz, nc, L, g, ns).transpose(1, 0, 2, 3, 4)
    cs = cm.reshape(bsz, nc, L, g, ns).transpose(1, 0, 2, 3, 4)
    ag = a.reshape(g, r)
    lower = jnp.tril(jnp.ones((L, L), dtype=bool))[None, :, :, None, None]

    def step(state, inp):
        xc, dtc, bc, cc = inp
        acum = jnp.cumsum(dtc * ag, axis=1)
        diff = acum[:, :, None] - acum[:, None, :]
        decay = jnp.exp(jnp.where(lower, diff, -jnp.inf))
        cb = jnp.einsum('btgn,bsgn->btsg', cc, bc)
        wts = cb[..., None] * decay * dtc[:, None]
        y_intra = jnp.einsum('btsgr,bsgrp->btgrp', wts, xc)
        y_state = jnp.einsum('btgn,bgrpn->btgrp', cc, state) * jnp.exp(acum)[..., None]
        dec_end = jnp.exp(acum[:, -1:] - acum) * dtc
        new_state = state * jnp.exp(acum[:, -1])[..., None, None] + jnp.einsum('bsgr,bsgrp,bsgn->bgrpn', dec_end, xc, bc)
        return new_state, y_intra + y_state

    state0 = jnp.zeros((bsz, g, r, hp, ns), jnp.float32)
    _, ys = lax.scan(step, state0, (xs, dts, bs, cs))
    return ys.transpose(1, 0, 2, 3, 4, 5).reshape(bsz, s, nh, hp)


def token_mixer(h, w_in, ssd_conv_w, ssd_conv_b, dt_bias_fwd, dt_bias_bwd, a_log_fwd, a_log_bwd,
                ssd_d, ssd_norm_w, w_attn_proj, w_ssd_proj, w_out):
    bsz, s, _ = h.shape
    proj = h @ w_in
    q, k, v, z, xbc, dt_raw, g_attn, g_ssd = jnp.split(proj, IN_OFFSETS, axis=-1)

    cos, sin = rotary_tables(s)
    q = apply_rotary(q.reshape(bsz, s, N_ATTN_HEADS, HEAD_DIM), cos, sin)
    k = apply_rotary(k.reshape(bsz, s, N_ATTN_HEADS, HEAD_DIM), cos, sin)
    v = v.reshape(bsz, s, N_ATTN_HEADS, HEAD_DIM)
    outs, lses = [], []
    for gi, (window, dilation) in enumerate(ATTN_GROUPS):
        sl = slice(gi * HEADS_PER_GROUP, (gi + 1) * HEADS_PER_GROUP)
        o, l = dilated_window_attention(q[:, :, sl], k[:, :, sl], v[:, :, sl], dilation, window // (2 * dilation))
        outs.append(o)
        lses.append(l)
    mix_w = jax.nn.softmax(jnp.stack(lses, axis=0), axis=0)
    attn = jnp.sum(mix_w[..., None] * jnp.stack(outs, axis=0), axis=0)
    a_branch = attn.reshape(bsz, s, ATTN_OUT_WIDTH).astype(h.dtype) @ w_attn_proj

    xbc = jax.nn.silu(depthwise_conv_centred(xbc, ssd_conv_w, ssd_conv_b)).astype(jnp.float32)
    xs, bm, cm = jnp.split(xbc, [SSD_INNER, SSD_INNER + SSD_GROUPS * SSD_STATE], axis=-1)
    xs = xs.reshape(bsz, s, SSD_HEADS, SSD_HEAD_DIM)
    bm = bm.reshape(bsz, s, SSD_GROUPS, SSD_STATE)
    cm = cm.reshape(bsz, s, SSD_GROUPS, SSD_STATE)
    dt_f, dt_b = jnp.split(dt_raw.astype(jnp.float32), 2, axis=-1)
    dt_f = jax.nn.softplus(dt_f + dt_bias_fwd.astype(jnp.float32))
    dt_b = jax.nn.softplus(dt_b + dt_bias_bwd.astype(jnp.float32))
    a_f = -jnp.exp(a_log_fwd.astype(jnp.float32))
    a_b = -jnp.exp(a_log_bwd.astype(jnp.float32))
    flip = lambda t: jnp.flip(t, axis=1)
    y_fwd = ssd_chunked_scan(xs, dt_f, a_f, bm, cm)
    y_bwd = flip(ssd_chunked_scan(flip(xs), flip(dt_b), a_b, flip(bm), flip(cm)))
    y = y_fwd + y_bwd + ssd_d.astype(jnp.float32)[:, None] * xs
    y = y.reshape(bsz, s, SSD_INNER) * jax.nn.silu(z.astype(jnp.float32))
    y = group_rmsnorm(y, ssd_norm_w, SSD_GROUPS)
    s_branch = y.astype(h.dtype) @ w_ssd_proj

    merged = jax.nn.sigmoid(g_attn) * a_branch + jax.nn.sigmoid(g_ssd) * s_branch
    return merged @ w_out


def conv_ffn(h, w_up, ffn_conv_w, ffn_conv_b, w_down):
    up = h @ w_up
    gate, val = jnp.split(up, [D_FF], axis=-1)
    gate = depthwise_conv_centred(gate, ffn_conv_w, ffn_conv_b)
    return (jax.nn.gelu(gate, approximate=False) * val) @ w_down


def encoder_trunk(x, c, w_ada, b_ada, norm_mix_w, w_in, ssd_conv_w, ssd_conv_b, dt_bias_fwd, dt_bias_bwd,
                  a_log_fwd, a_log_bwd, ssd_d, ssd_norm_w, w_attn_proj, w_ssd_proj, w_out, norm_ffn_w,
                  w_up, ffn_conv_w, ffn_conv_b, w_down, norm_f_w):
    dtype = x.dtype
    for l in range(DEPTH):
        mod = (jax.nn.silu(c) @ w_ada[l] + b_ada[l])[:, None, :].astype(jnp.float32)
        sh_m, sc_m, g_m, sh_f, sc_f, g_f = jnp.split(mod, 6, axis=-1)
        h = (rmsnorm(x, norm_mix_w[l]) * (1.0 + sc_m) + sh_m).astype(dtype)
        mix = token_mixer(h, w_in[l], ssd_conv_w[l], ssd_conv_b[l], dt_bias_fwd[l], dt_bias_bwd[l],
                          a_log_fwd[l], a_log_bwd[l], ssd_d[l], ssd_norm_w[l], w_attn_proj[l],
                          w_ssd_proj[l], w_out[l])
        x = (x + g_m * mix).astype(dtype)
        h = (rmsnorm(x, norm_ffn_w[l]) * (1.0 + sc_f) + sh_f).astype(dtype)
        x = (x + g_f * conv_ffn(h, w_up[l], ffn_conv_w[l], ffn_conv_b[l], w_down[l])).astype(dtype)
    return rmsnorm(x, norm_f_w).astype(dtype)


def setup_inputs(seed: int = 0) -> dict:
    key = jax.random.key(seed)
    ks = jax.random.split(key, 32)
    f32 = jnp.float32

    def nrm(k, shape, scale):
        return jax.random.normal(k, shape, f32) * scale

    dt0 = jnp.exp(jax.random.uniform(ks[12], (DEPTH, 2, SSD_HEADS), f32) * (np.log(0.1) - np.log(0.001)) + np.log(0.001))
    dt_bias = dt0 + jnp.log(-jnp.expm1(-dt0))
    a_log = jnp.log(jax.random.uniform(ks[13], (DEPTH, 2, SSD_HEADS), f32, 1.0, 16.0))
    return {
        "x_prompt": nrm(ks[0], (BATCH, SEQ, D_MODEL), 1.0),
        "x_sample": nrm(ks[1], (DEC_BATCH, DEC_SEQ, D_MODEL), 1.0),
        "c_prompt": nrm(ks[2], (BATCH, D_MODEL), 1.0),
        "c_sample": nrm(ks[3], (DEC_BATCH, D_MODEL), 1.0),
        "w_ada": nrm(ks[4], (DEPTH, D_MODEL, 6 * D_MODEL), 0.5 * D_MODEL ** -0.5),
        "b_ada": nrm(ks[5], (DEPTH, 6 * D_MODEL), 0.02),
        "norm_mix_w": 1.0 + nrm(ks[6], (DEPTH, D_MODEL), 0.05),
        "w_in": nrm(ks[7], (DEPTH, D_MODEL, IN_WIDTH), D_MODEL ** -0.5),
        "ssd_conv_w": nrm(ks[8], (DEPTH, SSD_CONV, SSD_CONV_DIM), SSD_CONV ** -0.5),
        "ssd_conv_b": nrm(ks[9], (DEPTH, SSD_CONV_DIM), 0.02),
        "dt_bias_fwd": dt_bias[:, 0],
        "dt_bias_bwd": dt_bias[:, 1],
        "a_log_fwd": a_log[:, 0],
        "a_log_bwd": a_log[:, 1],
        "ssd_d": 1.0 + nrm(ks[14], (DEPTH, SSD_HEADS), 0.1),
        "ssd_norm_w": 1.0 + nrm(ks[15], (DEPTH, SSD_INNER), 0.05),
        "w_attn_proj": nrm(ks[16], (DEPTH, ATTN_OUT_WIDTH, D_MODEL), ATTN_OUT_WIDTH ** -0.5),
        "w_ssd_proj": nrm(ks[17], (DEPTH, SSD_INNER, D_MODEL), SSD_INNER ** -0.5),
        "w_out": nrm(ks[18], (DEPTH, D_MODEL, D_MODEL), D_MODEL ** -0.5),
        "norm_ffn_w": 1.0 + nrm(ks[19], (DEPTH, D_MODEL), 0.05),
        "w_up": nrm(ks[20], (DEPTH, D_MODEL, 2 * D_FF), D_MODEL ** -0.5),
        "ffn_conv_w": nrm(ks[21], (DEPTH, FFN_CONV, D_FF), FFN_CONV ** -0.5),
        "ffn_conv_b": nrm(ks[22], (DEPTH, D_FF), 0.02),
        "w_down": nrm(ks[23], (DEPTH, D_FF, D_MODEL), D_FF ** -0.5),
        "norm_f_w": 1.0 + nrm(ks[24], (D_MODEL,), 0.05),
    }


def reference(x_prompt, x_sample, c_prompt, c_sample, w_ada, b_ada, norm_mix_w, w_in, ssd_conv_w, ssd_conv_b,
              dt_bias_fwd, dt_bias_bwd, a_log_fwd, a_log_bwd, ssd_d, ssd_norm_w, w_attn_proj, w_ssd_proj,
              w_out, norm_ffn_w, w_up, ffn_conv_w, ffn_conv_b, w_down, norm_f_w):
    y_prompt = encoder_trunk(x_prompt, c_prompt, w_ada, b_ada, norm_mix_w, w_in, ssd_conv_w, ssd_conv_b,
                             dt_bias_fwd, dt_bias_bwd, a_log_fwd, a_log_bwd, ssd_d, ssd_norm_w, w_attn_proj,
                             w_ssd_proj, w_out, norm_ffn_w, w_up, ffn_conv_w, ffn_conv_b, w_down, norm_f_w)
    y_sample = encoder_trunk(x_sample, c_sample, w_ada, b_ada, norm_mix_w, w_in, ssd_conv_w, ssd_conv_b,
                             dt_bias_fwd, dt_bias_bwd, a_log_fwd, a_log_bwd, ssd_d, ssd_norm_w, w_attn_proj,
                             w_ssd_proj, w_out, norm_ffn_w, w_up, ffn_conv_w, ffn_conv_b, w_down, norm_f_w)
    return (y_prompt, y_sample)
```

```python
import functools

import numpy as np
import jax
import jax.numpy as jnp
from jax import lax
from jax.experimental import pallas as pl
from jax.experimental.pallas import tpu as pltpu

F32 = jnp.float32
BF16 = jnp.bfloat16

D_MODEL = 2048
HEAD_DIM = 128
ATTN_GROUPS = ((128, 1), (512, 4), (2048, 16))
HEADS_PER_GROUP = 4
N_ATTN_HEADS = HEADS_PER_GROUP * len(ATTN_GROUPS)
ATTN_WIDTH = N_ATTN_HEADS * HEAD_DIM
ATTN_OUT_WIDTH = HEADS_PER_GROUP * HEAD_DIM
ROPE_THETA = 10000.0
SSD_INNER = 2 * D_MODEL
SSD_HEAD_DIM = 64
SSD_HEADS = SSD_INNER // SSD_HEAD_DIM
SSD_GROUPS = 8
SSD_HEADS_PER_GROUP = SSD_HEADS // SSD_GROUPS
SSD_STATE = 128
SSD_CONV = 5
SSD_CHUNK = 128
SSD_BC = 2 * SSD_GROUPS * SSD_STATE
D_FF = 5632
FFN_CONV = 3
EPS = 1e-6
NEG_INF = -1e30

QKV_WIDTH = 3 * ATTN_WIDTH
REST_WIDTH = 2 * SSD_INNER + SSD_BC + 2 * D_MODEL
REST_Z, REST_XS, REST_BC, REST_GA, REST_GS = 0, SSD_INNER, 2 * SSD_INNER, 2 * SSD_INNER + SSD_BC, 2 * SSD_INNER + SSD_BC + D_MODEL

LANES = 128
HALO = 16
VMEM_LIMIT = 56 * 1024 * 1024
TM = 1024


def _cparams(sem):
    return pltpu.CompilerParams(dimension_semantics=sem, vmem_limit_bytes=VMEM_LIMIT)


def _silu(x):
    return x * jax.nn.sigmoid(x)


def _mod_kernel(c_ref, w_ref, b_ref, o_ref):
    c = c_ref[...]
    o_ref[...] = jnp.dot(_silu(c), w_ref[...], preferred_element_type=F32,
                         precision=lax.Precision.HIGHEST) + b_ref[...]


def _modulation(c_all, w_ada, b_ada):
    rows = c_all.shape[0]
    tn = 1024
    return pl.pallas_call(
        _mod_kernel,
        grid=(6 * D_MODEL // tn,),
        in_specs=[pl.BlockSpec((rows, D_MODEL), lambda j: (0, 0)),
                  pl.BlockSpec((D_MODEL, tn), lambda j: (0, j)),
                  pl.BlockSpec((1, tn), lambda j: (0, j))],
        out_specs=pl.BlockSpec((rows, tn), lambda j: (0, j)),
        out_shape=jax.ShapeDtypeStruct((rows, 6 * D_MODEL), F32),
        compiler_params=_cparams(("arbitrary",)),
        name="mod",
    )(c_all, w_ada, b_ada.reshape(1, -1))


def _rms_mod(x, w, scale, shift):
    y = x * lax.rsqrt(jnp.mean(x * x, axis=-1, keepdims=True) + EPS) * w
    return y * (1.0 + scale) + shift


def _norm_kernel(x_ref, mod_ref, w_ref, h_ref):
    h_ref[0] = _rms_mod(x_ref[0], w_ref[...], mod_ref[0, 1:2, :], mod_ref[0, 0:1, :]).astype(BF16)


def _norm_mod(x, mod, w):
    bsz, s, _ = x.shape
    ts = 512
    return pl.pallas_call(
        _norm_kernel,
        grid=(bsz, s // ts),
        in_specs=[pl.BlockSpec((1, ts, D_MODEL), lambda b, i: (b, i, 0)),
                  pl.BlockSpec((1, 6, D_MODEL), lambda b, i: (b, 0, 0)),
                  pl.BlockSpec((1, D_MODEL), lambda b, i: (0, 0))],
        out_specs=pl.BlockSpec((1, ts, D_MODEL), lambda b, i: (b, i, 0)),
        out_shape=jax.ShapeDtypeStruct((bsz, s, D_MODEL), BF16),
        compiler_params=_cparams(("parallel", "parallel")),
        name="norm_mix",
    )(x, mod, w.reshape(1, -1))


QKV_TN = 6 * HEAD_DIM


def _qkv_kernel(h_ref, w_ref, cos_ref, sin_ref, o_ref):
    acc = jnp.dot(h_ref[...], w_ref[...], preferred_element_type=F32)
    cos = cos_ref[...]
    sin = sin_ref[...]
    for t in range(QKV_TN // HEAD_DIM):
        sl = slice(t * HEAD_DIM, (t + 1) * HEAD_DIM)
        a = acc[:, sl]
        if t % 3 != 2:
            a = a * cos + pltpu.roll(a, HEAD_DIM // 2, 1) * sin
        o_ref[:, sl] = a.astype(BF16)


def _qkv_proj(h2d, w_qkv, cos_t, sin_t, s):
    m = h2d.shape[0]
    nst = s // TM
    return pl.pallas_call(
        _qkv_kernel,
        grid=(m // TM, QKV_WIDTH // QKV_TN),
        in_specs=[pl.BlockSpec((TM, D_MODEL), lambda i, j: (i, 0)),
                  pl.BlockSpec((D_MODEL, QKV_TN), lambda i, j: (0, j)),
                  pl.BlockSpec((TM, HEAD_DIM), lambda i, j: (i % nst, 0)),
                  pl.BlockSpec((TM, HEAD_DIM), lambda i, j: (i % nst, 0))],
        out_specs=pl.BlockSpec((TM, QKV_TN), lambda i, j: (i, j)),
        out_shape=jax.ShapeDtypeStruct((m, QKV_WIDTH), BF16),
        compiler_params=_cparams(("parallel", "arbitrary")),
        name="qkv_proj",
    )(h2d, w_qkv, cos_t, sin_t)


REST_TN = 1024


def _rest_kernel(h_ref, w_ref, wdt_ref, o_ref, dt_ref):
    o_ref[...] = jnp.dot(h_ref[...], w_ref[...], preferred_element_type=F32).astype(BF16)

    @pl.when(pl.program_id(1) == 0)
    def _():
        dt_ref[...] = jnp.dot(h_ref[...], wdt_ref[...], preferred_element_type=F32)


def _rest_proj(h2d, w_rest, w_dt):
    m = h2d.shape[0]
    return pl.pallas_call(
        _rest_kernel,
        grid=(m // TM, REST_WIDTH // REST_TN),
        in_specs=[pl.BlockSpec((TM, D_MODEL), lambda i, j: (i, 0)),
                  pl.BlockSpec((D_MODEL, REST_TN), lambda i, j: (0, j)),
                  pl.BlockSpec((D_MODEL, 2 * SSD_HEADS), lambda i, j: (0, 0))],
        out_specs=[pl.BlockSpec((TM, REST_TN), lambda i, j: (i, j)),
                   pl.BlockSpec((TM, 2 * SSD_HEADS), lambda i, j: (i, 0))],
        out_shape=[jax.ShapeDtypeStruct((m, REST_WIDTH), BF16),
                   jax.ShapeDtypeStruct((m, 2 * SSD_HEADS), F32)],
        compiler_params=_cparams(("parallel", "arbitrary")),
        name="rest_proj",
    )(h2d, w_rest, w_dt)


ATTN_TQ = 128
ATTN_RADIUS = 64


def _attn_kernel(qkv_ref, o_ref, l_ref, *, n, hps):
    tq = ATTN_TQ
    win = min(n, tq + 2 * ATTN_RADIUS)
    scale = HEAD_DIM ** -0.5

    def body(qi, carry):
        i0 = pl.multiple_of(qi * tq, tq)
        start = pl.multiple_of(jnp.clip(i0 - ATTN_RADIUS, 0, n - win), ATTN_RADIUS)
        rows = i0 + lax.broadcasted_iota(jnp.int32, (tq, win), 0)
        cols = start + lax.broadcasted_iota(jnp.int32, (tq, win), 1)
        valid = jnp.abs(cols - rows) <= ATTN_RADIUS
        for hh in range(hps):
            base = hh * 3 * HEAD_DIM
            q = qkv_ref[0, pl.ds(i0, tq), base:base + HEAD_DIM]
            k = qkv_ref[0, pl.ds(start, win), base + HEAD_DIM:base + 2 * HEAD_DIM]
            v = qkv_ref[0, pl.ds(start, win), base + 2 * HEAD_DIM:base + 3 * HEAD_DIM]
            sc = lax.dot_general(q, k, (((1,), (1,)), ((), ())), preferred_element_type=F32) * scale
            sc = jnp.where(valid, sc, NEG_INF)
            mx = jnp.max(sc, axis=-1, keepdims=True)
            p = jnp.exp(sc - mx)
            den = jnp.sum(p, axis=-1, keepdims=True)
            o = jnp.dot(p.astype(BF16), v, preferred_element_type=F32) / den
            osl = slice(hh * HEAD_DIM, (hh + 1) * HEAD_DIM)
            o_ref[0, pl.ds(i0, tq), osl] = o.astype(o_ref.dtype)
            l_ref[0, pl.ds(i0, tq), osl] = jnp.broadcast_to(mx + jnp.log(den), (tq, HEAD_DIM))
        return carry

    lax.fori_loop(0, n // tq, body, 0)


def _attention_group(qkv, gi, dilation):
    bsz, s, _ = qkv.shape
    n = s // dilation
    hps = max(1, min(HEADS_PER_GROUP, 8192 // n))
    hsteps = HEADS_PER_GROUP // hps
    in_w = 3 * HEAD_DIM * hps
    out_w = HEAD_DIM * hps
    qkv_v = qkv.reshape(bsz, n, dilation * QKV_WIDTH)
    row_blocks = QKV_WIDTH // in_w
    grp_off = gi * (3 * ATTN_OUT_WIDTH) // in_w
    o, l = pl.pallas_call(
        functools.partial(_attn_kernel, n=n, hps=hps),
        grid=(bsz, dilation, hsteps),
        in_specs=[pl.BlockSpec((1, n, in_w), lambda b, r, hh: (b, 0, r * row_blocks + grp_off + hh))],
        out_specs=[pl.BlockSpec((1, n, out_w), lambda b, r, hh: (b, 0, r * hsteps + hh)),
                   pl.BlockSpec((1, n, out_w), lambda b, r, hh: (b, 0, r * hsteps + hh))],
        out_shape=[jax.ShapeDtypeStruct((bsz, n, dilation * ATTN_OUT_WIDTH), BF16),
                   jax.ShapeDtypeStruct((bsz, n, dilation * ATTN_OUT_WIDTH), F32)],
        compiler_params=_cparams(("parallel", "parallel", "parallel")),
        name=f"attn_d{dilation}",
    )(qkv_v)
    return o.reshape(bsz, s, ATTN_OUT_WIDTH), l.reshape(bsz, s, ATTN_OUT_WIDTH)


def _ssd_kernel(*refs, reverse, final):
    L, P, N, R = SSD_CHUNK, SSD_HEAD_DIM, SSD_STATE, SSD_HEADS_PER_GROUP
    if final:
        (z_ref, xs_ref, xsp_ref, xsn_ref, bc_ref, bcp_ref, bcn_ref, dt_ref, cwx_ref, cbx_ref, cwb_ref, cbb_ref,
         dtb_ref, alog_ref, dskip_ref, nw_ref, sb_ref, y_ref,
         state_ref, ext_ref, xc_ref, xt_ref, bm_ref, cm_ref, at_ref, dtt_ref, dec_ref, tot_ref, yt_ref) = refs
    else:
        (xs_ref, xsp_ref, xsn_ref, bc_ref, bcp_ref, bcn_ref, dt_ref, cwx_ref, cbx_ref, cwb_ref, cbb_ref,
         dtb_ref, alog_ref, sout_ref,
         state_ref, ext_ref, xc_ref, xt_ref, bm_ref, at_ref, dtt_ref, dec_ref, tot_ref) = refs
    c = pl.program_id(1)
    nc = pl.num_programs(1)
    cc = nc - 1 - c if reverse else c
    dir_off = SSD_HEADS if reverse else 0

    @pl.when(c == 0)
    def _():
        state_ref[...] = jnp.zeros_like(state_ref)

    keep_prev = jnp.where(cc > 0, 1.0, 0.0).astype(F32)
    keep_next = jnp.where(cc < nc - 1, 1.0, 0.0).astype(F32)

    def conv_silu(main_ref, prev_ref, next_ref, w_ref, b_ref, col0, width):
        sl = slice(col0, col0 + width)
        ext_ref[0:HALO, 0:width] = prev_ref[0, :, sl].astype(F32) * keep_prev
        ext_ref[HALO:HALO + L, 0:width] = main_ref[0, :, sl].astype(F32)
        ext_ref[HALO + L:HALO + L + HALO, 0:width] = next_ref[0, :, sl].astype(F32) * keep_next
        acc = ext_ref[pl.ds(HALO - SSD_CONV // 2, L), 0:width] * w_ref[0:1, sl]
        for j in range(1, SSD_CONV):
            acc = acc + ext_ref[pl.ds(HALO - SSD_CONV // 2 + j, L), 0:width] * w_ref[j:j + 1, sl]
        acc = acc + b_ref[:, sl]
        return _silu(acc)

    gw = R * P
    for g in range(SSD_GROUPS):
        xg = conv_silu(xs_ref, xsp_ref, xsn_ref, cwx_ref, cbx_ref, g * gw, gw)
        if final:
            xc_ref[:, g * gw:(g + 1) * gw] = xg
        for q in range(gw // LANES):
            blk = g * (gw // LANES) + q
            xt_ref[blk * LANES:(blk + 1) * LANES, :] = xg[:, q * LANES:(q + 1) * LANES].T
    for q in range(SSD_BC // gw if final else SSD_BC // (2 * gw)):
        bcg = conv_silu(bc_ref, bcp_ref, bcn_ref, cwb_ref, cbb_ref, q * gw, gw)
        for t in range(gw // N):
            gidx = q * (gw // N) + t
            if gidx < SSD_GROUPS:
                bm_ref[gidx] = bcg[:, t * N:(t + 1) * N].astype(BF16)
            elif final:
                cm_ref[gidx - SSD_GROUPS] = bcg[:, t * N:(t + 1) * N]

    dt = jax.nn.softplus(dt_ref[0] + dtb_ref[...])
    dta = dt * (-jnp.exp(alog_ref[...]))
    row = lax.broadcasted_iota(jnp.int32, (L, L), 0)
    col = lax.broadcasted_iota(jnp.int32, (L, L), 1)
    cum_f = jnp.dot((col <= row).astype(F32), dta, preferred_element_type=F32, precision=lax.Precision.HIGHEST)
    cum_b = jnp.dot((col >= row).astype(F32), dta, preferred_element_type=F32, precision=lax.Precision.HIGHEST)
    lane = lax.broadcasted_iota(jnp.int32, (L, 2 * SSD_HEADS), 1)
    acum_t = jnp.where(lane < SSD_HEADS, cum_f, cum_b).T
    dt_t = dt.T
    hrow = lax.broadcasted_iota(jnp.int32, (2 * SSD_HEADS, 1), 0)
    tot = jnp.where(hrow < SSD_HEADS, acum_t[:, L - 1:L], acum_t[:, 0:1])
    at_ref[...] = acum_t
    dtt_ref[...] = dt_t
    dec_ref[...] = jnp.exp(tot - acum_t) * dt_t
    tot_ref[...] = jnp.broadcast_to(jnp.exp(tot), (2 * SSD_HEADS, N))

    if not final:
        sout_ref[0, 0] = state_ref[...].astype(BF16)

    def bcast_row(ref, r, rows):
        return jnp.broadcast_to(ref[pl.ds(r, 1), :], (rows, ref.shape[1]))

    def group_body(g, carry):
        b_g = bm_ref[g]
        if final:
            c_g = cm_ref[g]
            cb = lax.dot_general(c_g.astype(BF16), b_g, (((1,), (1,)), ((), ())), preferred_element_type=F32)
        for r in range(R):
            h = g * R + r
            hrows = pl.ds(pl.multiple_of(h * P, P), P)
            x_t = xt_ref[hrows, :]
            s_old = state_ref[hrows, :]
            if final:
                hf = h
                hb = h + SSD_HEADS
                arow_f = bcast_row(at_ref, hf, L)
                arow_b = bcast_row(at_ref, hb, L)
                acol_f = arow_f.T
                acol_b = arow_b.T
                d_f = jnp.exp(jnp.where(row >= col, acol_f - arow_f, NEG_INF)) * bcast_row(dtt_ref, hf, L)
                d_b = jnp.exp(jnp.where(row <= col, acol_b - arow_b, NEG_INF)) * bcast_row(dtt_ref, hb, L)
                w = (cb * (d_f + d_b)).astype(BF16)
                ec_f = (c_g * jnp.exp(acol_f)).astype(BF16)
                ec_b = (c_g * jnp.exp(acol_b)).astype(BF16)
                nt = (((1,), (1,)), ((), ()))
                y_t = lax.dot_general(x_t.astype(BF16), w, nt, preferred_element_type=F32)
                y_t = y_t + lax.dot_general(s_old.astype(BF16), ec_f, nt, preferred_element_type=F32)
                y_t = y_t + lax.dot_general(sb_ref[0, 0, hrows, :], ec_b, nt, preferred_element_type=F32)
                yt_ref[hrows, :] = y_t
            hd = h + dir_off
            xsc = (x_t * bcast_row(dec_ref, hd, P)).astype(BF16)
            state_ref[hrows, :] = s_old * bcast_row(tot_ref, hd, P) + jnp.dot(xsc, b_g, preferred_element_type=F32)
        return carry

    lax.fori_loop(0, SSD_GROUPS, group_body, 0)

    if final:
        for g in range(SSD_GROUPS):
            ys = []
            ssq = None
            for q in range(gw // LANES):
                blk = g * (gw // LANES) + q
                sl = slice(blk * LANES, (blk + 1) * LANES)
                yb = yt_ref[sl, :].T + xc_ref[:, sl] * dskip_ref[:, sl]
                yb = yb * _silu(z_ref[0, :, sl].astype(F32))
                ys.append(yb)
                part = jnp.sum(yb * yb, axis=-1, keepdims=True)
                ssq = part if ssq is None else ssq + part
            inv = lax.rsqrt(ssq * (1.0 / gw) + EPS)
            for q, yb in enumerate(ys):
                blk = g * (gw // LANES) + q
                sl = slice(blk * LANES, (blk + 1) * LANES)
                y_ref[0, :, sl] = (yb * inv * nw_ref[:, sl]).astype(BF16)


def _ssd(rest, dt_raw, conv_w, conv_b, dt_bias, a_log, d_skip, norm_w):
    bsz, s, _ = rest.shape
    L = SSD_CHUNK
    nc = s // L
    hb = L // HALO
    n_halo = s // HALO
    cwx, cwb = conv_w[:, :SSD_INNER], conv_w[:, SSD_INNER:]
    cbx, cbb = conv_b[:, :SSD_INNER], conv_b[:, SSD_INNER:]

    def specs(chunk):
        prev = lambda b, c: (b, jnp.maximum(chunk(c) * hb - 1, 0), REST_XS // SSD_INNER)
        nxt = lambda b, c: (b, jnp.minimum((chunk(c) + 1) * hb, n_halo - 1), REST_XS // SSD_INNER)
        prev_bc = lambda b, c: (b, jnp.maximum(chunk(c) * hb - 1, 0), REST_BC // SSD_BC)
        nxt_bc = lambda b, c: (b, jnp.minimum((chunk(c) + 1) * hb, n_halo - 1), REST_BC // SSD_BC)
        const = lambda b, c: (0, 0)
        return [
            pl.BlockSpec((1, L, SSD_INNER), lambda b, c: (b, chunk(c), REST_XS // SSD_INNER)),
            pl.BlockSpec((1, HALO, SSD_INNER), prev),
            pl.BlockSpec((1, HALO, SSD_INNER), nxt),
            pl.BlockSpec((1, L, SSD_BC), lambda b, c: (b, chunk(c), REST_BC // SSD_BC)),
            pl.BlockSpec((1, HALO, SSD_BC), prev_bc),
            pl.BlockSpec((1, HALO, SSD_BC), nxt_bc),
            pl.BlockSpec((1, L, 2 * SSD_HEADS), lambda b, c: (b, chunk(c), 0)),
            pl.BlockSpec((SSD_CONV, SSD_INNER), const),
            pl.BlockSpec((1, SSD_INNER), const),
            pl.BlockSpec((SSD_CONV, SSD_BC), const),
            pl.BlockSpec((1, SSD_BC), const),
            pl.BlockSpec((1, 2 * SSD_HEADS), const),
            pl.BlockSpec((1, 2 * SSD_HEADS), const),
        ]

    common_scratch = [
        pltpu.VMEM((SSD_INNER, SSD_STATE), F32),
        pltpu.VMEM((L + 2 * HALO, SSD_HEADS_PER_GROUP * SSD_HEAD_DIM), F32),
        pltpu.VMEM((L, SSD_INNER), F32),
        pltpu.VMEM((SSD_INNER, L), F32),
        pltpu.VMEM((SSD_GROUPS, L, SSD_STATE), BF16),
    ]
    tail_scratch = [
        pltpu.VMEM((2 * SSD_HEADS, L), F32),
        pltpu.VMEM((2 * SSD_HEADS, L), F32),
        pltpu.VMEM((2 * SSD_HEADS, L), F32),
        pltpu.VMEM((2 * SSD_HEADS, SSD_STATE), F32),
    ]
    args = (rest, rest, rest, rest, rest, rest, dt_raw, cwx, cbx, cwb, cbb, dt_bias, a_log)

    rev = lambda c: nc - 1 - c
    states_b = pl.pallas_call(
        functools.partial(_ssd_kernel, reverse=True, final=False),
        grid=(bsz, nc),
        in_specs=specs(rev),
        out_specs=pl.BlockSpec((1, 1, SSD_INNER, SSD_STATE), lambda b, c: (b, rev(c), 0, 0)),
        out_shape=jax.ShapeDtypeStruct((bsz, nc, SSD_INNER, SSD_STATE), BF16),
        scratch_shapes=common_scratch + tail_scratch,
        compiler_params=_cparams(("parallel", "arbitrary")),
        name="ssd_bwd_states",
    )(*args)

    fwd = lambda c: c
    const = lambda b, c: (0, 0)
    y = pl.pallas_call(
        functools.partial(_ssd_kernel, reverse=False, final=True),
        grid=(bsz, nc),
        in_specs=[pl.BlockSpec((1, L, SSD_INNER), lambda b, c: (b, c, REST_Z // SSD_INNER))] + specs(fwd) + [
            pl.BlockSpec((1, SSD_INNER), const),
            pl.BlockSpec((1, SSD_INNER), const),
            pl.BlockSpec((1, 1, SSD_INNER, SSD_STATE), lambda b, c: (b, c, 0, 0)),
        ],
        out_specs=pl.BlockSpec((1, L, SSD_INNER), lambda b, c: (b, c, 0)),
        out_shape=jax.ShapeDtypeStruct((bsz, s, SSD_INNER), BF16),
        scratch_shapes=common_scratch + [pltpu.VMEM((SSD_GROUPS, L, SSD_STATE), F32)] + tail_scratch
        + [pltpu.VMEM((SSD_INNER, L), F32)],
        compiler_params=_cparams(("parallel", "arbitrary")),
        name="ssd_fwd",
    )(rest, *args, d_skip, norm_w, states_b)
    return y


MIX_TN = 512


def _mix_kernel(o0, o1, o2, l0, l1, l2, y_ref, ga_ref, gs_ref, wa_ref, ws_ref, m_ref, attn_ref):
    @pl.when(pl.program_id(1) == 0)
    def _():
        ls = [l0[...], l1[...], l2[...]]
        mx = jnp.maximum(jnp.maximum(ls[0], ls[1]), ls[2])
        es = [jnp.exp(l - mx) for l in ls]
        den = es[0] + es[1] + es[2]
        num = es[0] * o0[...].astype(F32) + es[1] * o1[...].astype(F32) + es[2] * o2[...].astype(F32)
        attn_ref[...] = (num / den).astype(BF16)

    a_br = jnp.dot(attn_ref[...], wa_ref[...], preferred_element_type=F32)
    s_br = jnp.dot(y_ref[...], ws_ref[...], preferred_element_type=F32)
    merged = jax.nn.sigmoid(ga_ref[...].astype(F32)) * a_br + jax.nn.sigmoid(gs_ref[...].astype(F32)) * s_br
    m_ref[...] = merged.astype(BF16)


def _mix(attn_outs, attn_lses, y2d, rest2d, w_attn_proj, w_ssd_proj):
    m = y2d.shape[0]
    tm = 512
    aspec = pl.BlockSpec((tm, ATTN_OUT_WIDTH), lambda i, j: (i, 0))
    ga0 = REST_GA // MIX_TN
    gs0 = REST_GS // MIX_TN
    return pl.pallas_call(
        _mix_kernel,
        grid=(m // tm, D_MODEL // MIX_TN),
        in_specs=[aspec] * 6 + [
            pl.BlockSpec((tm, SSD_INNER), lambda i, j: (i, 0)),
            pl.BlockSpec((tm, MIX_TN), lambda i, j: (i, ga0 + j)),
            pl.BlockSpec((tm, MIX_TN), lambda i, j: (i, gs0 + j)),
            pl.BlockSpec((ATTN_OUT_WIDTH, MIX_TN), lambda i, j: (0, j)),
            pl.BlockSpec((SSD_INNER, MIX_TN), lambda i, j: (0, j)),
        ],
        out_specs=pl.BlockSpec((tm, MIX_TN), lambda i, j: (i, j)),
        out_shape=jax.ShapeDtypeStruct((m, D_MODEL), BF16),
        scratch_shapes=[pltpu.VMEM((tm, ATTN_OUT_WIDTH), BF16)],
        compiler_params=_cparams(("parallel", "arbitrary")),
        name="mix",
    )(*attn_outs, *attn_lses, y2d, rest2d, rest2d, w_attn_proj, w_ssd_proj)


def _out_kernel(m_ref, w_ref, x_ref, mod_ref, nw_ref, x1_ref, h2_ref):
    mix = jnp.dot(m_ref[0], w_ref[...], preferred_element_type=F32)
    x1 = x_ref[0] + mod_ref[0, 2:3, :] * mix
    x1_ref[0] = x1
    h2_ref[0] = _rms_mod(x1, nw_ref[...], mod_ref[0, 4:5, :], mod_ref[0, 3:4, :]).astype(BF16)


def _out_proj(merged, w_out, x, mod, norm_w):
    bsz, s, _ = x.shape
    ts = 256
    tok = pl.BlockSpec((1, ts, D_MODEL), lambda b, i: (b, i, 0))
    return pl.pallas_call(
        _out_kernel,
        grid=(bsz, s // ts),
        in_specs=[tok,
                  pl.BlockSpec((D_MODEL, D_MODEL), lambda b, i: (0, 0)),
                  tok,
                  pl.BlockSpec((1, 6, D_MODEL), lambda b, i: (b, 0, 0)),
                  pl.BlockSpec((1, D_MODEL), lambda b, i: (0, 0))],
        out_specs=[tok, tok],
        out_shape=[jax.ShapeDtypeStruct((bsz, s, D_MODEL), F32),
                   jax.ShapeDtypeStruct((bsz, s, D_MODEL), BF16)],
        compiler_params=_cparams(("parallel", "parallel")),
        name="out_proj",
    )(merged, w_out, x, mod, norm_w.reshape(1, -1))


UP_TN = 1024


def _up_kernel(h_ref, w_ref, o_ref):
    o_ref[...] = jnp.dot(h_ref[...], w_ref[...], preferred_element_type=F32).astype(BF16)


def _up_proj(h2d, w_up):
    m = h2d.shape[0]
    return pl.pallas_call(
        _up_kernel,
        grid=(m // TM, 2 * D_FF // UP_TN),
        in_specs=[pl.BlockSpec((TM, D_MODEL), lambda i, j: (i, 0)),
                  pl.BlockSpec((D_MODEL, UP_TN), lambda i, j: (0, j))],
        out_specs=pl.BlockSpec((TM, UP_TN), lambda i, j: (i, j)),
        out_shape=jax.ShapeDtypeStruct((m, 2 * D_FF), BF16),
        compiler_params=_cparams(("parallel", "arbitrary")),
        name="ffn_up",
    )(h2d, w_up)


DOWN_TM = 512
DOWN_TK = 512


def _down_kernel(g_ref, gp_ref, gn_ref, v_ref, cw_ref, cb_ref, w_ref, x1_ref, mod_ref, nw_ref, o_ref,
                 acc_ref, ext_ref, *, tiles_per_seq):
    i = pl.program_id(0)
    k = pl.program_id(1)
    tm = DOWN_TM

    @pl.when(k == 0)
    def _():
        acc_ref[...] = jnp.zeros_like(acc_ref)

    keep_prev = jnp.where(i % tiles_per_seq > 0, 1.0, 0.0).astype(F32)
    keep_next = jnp.where(i % tiles_per_seq < tiles_per_seq - 1, 1.0, 0.0).astype(F32)
    ext_ref[0:HALO, :] = gp_ref[...].astype(F32) * keep_prev
    ext_ref[HALO:HALO + tm, :] = g_ref[...].astype(F32)
    ext_ref[HALO + tm:HALO + tm + HALO, :] = gn_ref[...].astype(F32) * keep_next
    gate = cb_ref[0]
    for j in range(FFN_CONV):
        gate = gate + ext_ref[pl.ds(HALO - FFN_CONV // 2 + j, tm), :] * cw_ref[0, j:j + 1, :]
    act = 0.5 * gate * (1.0 + lax.erf(gate * (2.0 ** -0.5))) * v_ref[...].astype(F32)
    acc_ref[...] += jnp.dot(act.astype(BF16), w_ref[...], preferred_element_type=F32)

    @pl.when(k == pl.num_programs(1) - 1)
    def _():
        x2 = x1_ref[...] + mod_ref[0, 5:6, :] * acc_ref[...]
        o_ref[...] = x2 * lax.rsqrt(jnp.mean(x2 * x2, axis=-1, keepdims=True) + EPS) * nw_ref[...]


def _down_proj(up2d, conv_w, conv_b, w_down, x1_2d, mod, norm_w, s):
    m = up2d.shape[0]
    tm, tk = DOWN_TM, DOWN_TK
    nk = D_FF // tk
    tiles_per_seq = s // tm
    hb = tm // HALO
    n_halo = m // HALO
    cw = conv_w.reshape(FFN_CONV, nk, tk).transpose(1, 0, 2)
    cb = conv_b.reshape(nk, 1, tk)
    return pl.pallas_call(
        functools.partial(_down_kernel, tiles_per_seq=tiles_per_seq),
        grid=(m // tm, nk),
        in_specs=[pl.BlockSpec((tm, tk), lambda i, k: (i, k)),
                  pl.BlockSpec((HALO, tk), lambda i, k: (jnp.maximum(i * hb - 1, 0), k)),
                  pl.BlockSpec((HALO, tk), lambda i, k: (jnp.minimum((i + 1) * hb, n_halo - 1), k)),
                  pl.BlockSpec((tm, tk), lambda i, k: (i, nk + k)),
                  pl.BlockSpec((1, FFN_CONV, tk), lambda i, k: (k, 0, 0)),
                  pl.BlockSpec((1, 1, tk), lambda i, k: (k, 0, 0)),
                  pl.BlockSpec((tk, D_MODEL), lambda i, k: (k, 0)),
                  pl.BlockSpec((tm, D_MODEL), lambda i, k: (i, 0)),
                  pl.BlockSpec((1, 6, D_MODEL), lambda i, k: (i // tiles_per_seq, 0, 0)),
                  pl.BlockSpec((1, D_MODEL), lambda i, k: (0, 0))],
        out_specs=pl.BlockSpec((tm, D_MODEL), lambda i, k: (i, 0)),
        out_shape=jax.ShapeDtypeStruct((m, D_MODEL), F32),
        scratch_shapes=[pltpu.VMEM((tm, D_MODEL), F32), pltpu.VMEM((tm + 2 * HALO, tk), F32)],
        compiler_params=_cparams(("parallel", "arbitrary")),
        name="ffn_down",
    )(up2d, up2d, up2d, up2d, cw, cb, w_down, x1_2d, mod, norm_w.reshape(1, -1))


def _rotary_tables(s):
    pos = jnp.arange(s, dtype=F32)
    inv_freq = ROPE_THETA ** (-jnp.arange(0, HEAD_DIM, 2, dtype=F32) / HEAD_DIM)
    ang = pos[:, None] * inv_freq[None, :]
    cos, sin = jnp.cos(ang), jnp.sin(ang)
    return jnp.concatenate([cos, cos], axis=-1), jnp.concatenate([-sin, sin], axis=-1)


def _prep_weights(w_in, w_attn_proj, w_ssd_proj, w_out, w_up, w_down):
    o = np.cumsum((0, ATTN_WIDTH, ATTN_WIDTH, ATTN_WIDTH, SSD_INNER, SSD_INNER + SSD_BC, 2 * SSD_HEADS,
                   D_MODEL, D_MODEL))
    wq, wk, wv = (w_in[:, o[t]:o[t + 1]].reshape(D_MODEL, N_ATTN_HEADS, HEAD_DIM) for t in range(3))
    w_qkv = jnp.stack([wq, wk, wv], axis=2).reshape(D_MODEL, QKV_WIDTH)
    w_rest = jnp.concatenate([w_in[:, o[3]:o[5]], w_in[:, o[6]:o[8]]], axis=1)
    w_dt = w_in[:, o[5]:o[6]]
    cast = lambda w: w.astype(BF16)
    return (cast(w_qkv), cast(w_rest), cast(w_dt), cast(w_attn_proj), cast(w_ssd_proj), cast(w_out),
            cast(w_up), cast(w_down))


def _trunk(x, mod, wts, small):
    bsz, s, _ = x.shape
    m = bsz * s
    w_qkv, w_rest, w_dt, w_attn_proj, w_ssd_proj, w_out, w_up, w_down = wts
    (norm_mix_w, ssd_conv_w, ssd_conv_b, dt_bias, a_log, d_skip, ssd_norm_w, norm_ffn_w, ffn_conv_w, ffn_conv_b,
     norm_f_w) = small
    h = _norm_mod(x, mod, norm_mix_w)
    h2d = h.reshape(m, D_MODEL)
    cos_t, sin_t = _rotary_tables(s)
    qkv = _qkv_proj(h2d, w_qkv, cos_t, sin_t, s).reshape(bsz, s, QKV_WIDTH)
    rest2d, dt_raw = _rest_proj(h2d, w_rest, w_dt)
    outs, lses = [], []
    for gi, (_, dilation) in enumerate(ATTN_GROUPS):
        o, l = _attention_group(qkv, gi, dilation)
        outs.append(o.reshape(m, ATTN_OUT_WIDTH))
        lses.append(l.reshape(m, ATTN_OUT_WIDTH))
    y = _ssd(rest2d.reshape(bsz, s, REST_WIDTH), dt_raw.reshape(bsz, s, 2 * SSD_HEADS), ssd_conv_w, ssd_conv_b,
             dt_bias, a_log, d_skip, ssd_norm_w)
    merged = _mix(outs, lses, y.reshape(m, SSD_INNER), rest2d, w_attn_proj, w_ssd_proj)
    x1, h2 = _out_proj(merged.reshape(bsz, s, D_MODEL), w_out, x, mod, norm_ffn_w)
    up = _up_proj(h2.reshape(m, D_MODEL), w_up)
    out = _down_proj(up, ffn_conv_w, ffn_conv_b, w_down, x1.reshape(m, D_MODEL), mod, norm_f_w, s)
    return out.reshape(bsz, s, D_MODEL)


def kernel(x_prompt, x_sample, c_prompt, c_sample, w_ada, b_ada, norm_mix_w, w_in, ssd_conv_w, ssd_conv_b,
           dt_bias_fwd, dt_bias_bwd, a_log_fwd, a_log_bwd, ssd_d, ssd_norm_w, w_attn_proj, w_ssd_proj, w_out,
           norm_ffn_w, w_up, ffn_conv_w, ffn_conv_b, w_down, norm_f_w):
    wts = _prep_weights(w_in[0], w_attn_proj[0], w_ssd_proj[0], w_out[0], w_up[0], w_down[0])
    small = (norm_mix_w[0], ssd_conv_w[0], ssd_conv_b[0].reshape(1, -1),
             jnp.concatenate([dt_bias_fwd[0], dt_bias_bwd[0]]).reshape(1, -1),
             jnp.concatenate([a_log_fwd[0], a_log_bwd[0]]).reshape(1, -1),
             jnp.repeat(ssd_d[0], SSD_HEAD_DIM).reshape(1, -1), ssd_norm_w[0].reshape(1, -1),
             norm_ffn_w[0], ffn_conv_w[0], ffn_conv_b[0], norm_f_w)
    nb_p = c_prompt.shape[0]
    nb_s = c_sample.shape[0]
    pad = (-(nb_p + nb_s)) % 8
    c_all = jnp.concatenate([c_prompt, c_sample, jnp.zeros((pad, D_MODEL), F32)], axis=0)
    mod_all = _modulation(c_all, w_ada[0], b_ada[0]).reshape(-1, 6, D_MODEL)
    y_prompt = _trunk(x_prompt, mod_all[:nb_p], wts, small)
    y_sample = _trunk(x_sample, mod_all[nb_p:nb_p + nb_s], wts, small)
    return (y_prompt, y_sample)
```

```python
import functools

import numpy as np
import jax
import jax.numpy as jnp
from jax import lax
from jax.experimental import pallas as pl
from jax.experimental.pallas import tpu as pltpu

F32 = jnp.float32
BF16 = jnp.bfloat16

D_MODEL = 2048
HEAD_DIM = 128
ATTN_GROUPS = ((128, 1), (512, 4), (2048, 16))
HEADS_PER_GROUP = 4
N_ATTN_HEADS = HEADS_PER_GROUP * len(ATTN_GROUPS)
ATTN_WIDTH = N_ATTN_HEADS * HEAD_DIM
ATTN_OUT_WIDTH = HEADS_PER_GROUP * HEAD_DIM
ROPE_THETA = 10000.0
SSD_INNER = 2 * D_MODEL
SSD_HEAD_DIM = 64
SSD_HEADS = SSD_INNER // SSD_HEAD_DIM
SSD_GROUPS = 8
SSD_HEADS_PER_GROUP = SSD_HEADS // SSD_GROUPS
SSD_STATE = 128
SSD_CONV = 5
SSD_CHUNK = 128
SSD_BC = 2 * SSD_GROUPS * SSD_STATE
D_FF = 5632
FFN_CONV = 3
EPS = 1e-6
NEG_INF = -1e30
LOG2E = 1.4426950408889634

QKV_WIDTH = 3 * ATTN_WIDTH
REST_WIDTH = 2 * SSD_INNER + SSD_BC + 2 * D_MODEL
REST_Z, REST_XS, REST_BC, REST_GA, REST_GS = 0, SSD_INNER, 2 * SSD_INNER, 2 * SSD_INNER + SSD_BC, 2 * SSD_INNER + SSD_BC + D_MODEL

LANES = 128
HALO = 16
VMEM_LIMIT = 56 * 1024 * 1024
TM = 1024


def _cparams(sem):
    return pltpu.CompilerParams(dimension_semantics=sem, vmem_limit_bytes=VMEM_LIMIT)


def _silu(x):
    return x * jax.nn.sigmoid(x)


def _mod_kernel(c_ref, w_ref, b_ref, o_ref):
    c = c_ref[...]
    o_ref[...] = jnp.dot(_silu(c), w_ref[...], preferred_element_type=F32,
                         precision=lax.Precision.HIGHEST) + b_ref[...]


def _modulation(c_all, w_ada, b_ada):
    rows = c_all.shape[0]
    tn = 1024
    return pl.pallas_call(
        _mod_kernel,
        grid=(6 * D_MODEL // tn,),
        in_specs=[pl.BlockSpec((rows, D_MODEL), lambda j: (0, 0)),
                  pl.BlockSpec((D_MODEL, tn), lambda j: (0, j)),
                  pl.BlockSpec((1, tn), lambda j: (0, j))],
        out_specs=pl.BlockSpec((rows, tn), lambda j: (0, j)),
        out_shape=jax.ShapeDtypeStruct((rows, 6 * D_MODEL), F32),
        compiler_params=_cparams(("arbitrary",)),
        name="mod",
    )(c_all, w_ada, b_ada.reshape(1, -1))


def _rms_mod(x, w, scale, shift):
    y = x * lax.rsqrt(jnp.mean(x * x, axis=-1, keepdims=True) + EPS) * w
    return y * (1.0 + scale) + shift


def _norm_kernel(x_ref, mod_ref, w_ref, h_ref):
    h_ref[0] = _rms_mod(x_ref[0], w_ref[...], mod_ref[0, 1:2, :], mod_ref[0, 0:1, :]).astype(BF16)


def _norm_mod(x, mod, w):
    bsz, s, _ = x.shape
    ts = 512
    return pl.pallas_call(
        _norm_kernel,
        grid=(bsz, s // ts),
        in_specs=[pl.BlockSpec((1, ts, D_MODEL), lambda b, i: (b, i, 0)),
                  pl.BlockSpec((1, 6, D_MODEL), lambda b, i: (b, 0, 0)),
                  pl.BlockSpec((1, D_MODEL), lambda b, i: (0, 0))],
        out_specs=pl.BlockSpec((1, ts, D_MODEL), lambda b, i: (b, i, 0)),
        out_shape=jax.ShapeDtypeStruct((bsz, s, D_MODEL), BF16),
        compiler_params=_cparams(("parallel", "parallel")),
        name="norm_mix",
    )(x, mod, w.reshape(1, -1))


QKV_TN = 6 * HEAD_DIM


QKV_GROUP_WIDTH = 3 * ATTN_OUT_WIDTH


def _qkv_kernel(h_ref, w_ref, cos_ref, sin_ref, o_ref, *scratch, dilation):
    acc = jnp.dot(h_ref[0], w_ref[...], preferred_element_type=F32)
    cos = cos_ref[...]
    sin = sin_ref[...]
    tiles = QKV_TN // HEAD_DIM
    for t in range(tiles):
        sl = slice(t * HEAD_DIM, (t + 1) * HEAD_DIM)
        a = acc[:, sl]
        if t % 3 != 2:
            a = a * cos + pltpu.roll(a, HEAD_DIM // 2, 1) * sin
        if dilation == 1:
            o_ref[0, 0, :, sl] = a.astype(BF16)
        else:
            scratch[0][t] = a
    if dilation > 1:
        rows = TM // dilation
        for r in range(dilation):
            for t in range(tiles):
                sl = slice(t * HEAD_DIM, (t + 1) * HEAD_DIM)
                o_ref[0, r, :, sl] = scratch[0][t, pl.ds(r, rows, stride=dilation), :].astype(BF16)


def _qkv_proj(h, w_qkv, cos_t, sin_t, gi, dilation):
    bsz, s, _ = h.shape
    col_tiles = QKV_GROUP_WIDTH // QKV_TN
    rows = TM // dilation
    return pl.pallas_call(
        functools.partial(_qkv_kernel, dilation=dilation),
        grid=(bsz, s // TM, col_tiles),
        in_specs=[pl.BlockSpec((1, TM, D_MODEL), lambda b, i, j: (b, i, 0)),
                  pl.BlockSpec((D_MODEL, QKV_TN), lambda b, i, j: (0, gi * col_tiles + j)),
                  pl.BlockSpec((TM, HEAD_DIM), lambda b, i, j: (i, 0)),
                  pl.BlockSpec((TM, HEAD_DIM), lambda b, i, j: (i, 0))],
        out_specs=pl.BlockSpec((1, dilation, rows, QKV_TN), lambda b, i, j: (b, 0, i, j)),
        out_shape=jax.ShapeDtypeStruct((bsz, dilation, s // dilation, QKV_GROUP_WIDTH), BF16),
        scratch_shapes=[pltpu.VMEM((QKV_TN // HEAD_DIM, TM, HEAD_DIM), F32)] if dilation > 1 else [],
        compiler_params=_cparams(("parallel", "parallel", "arbitrary")),
        name=f"qkv_proj_d{dilation}",
    )(h, w_qkv, cos_t, sin_t)


REST_TN = 1024


def _rest_kernel(h_ref, w_ref, wdt_ref, o_ref, dt_ref):
    o_ref[...] = jnp.dot(h_ref[...], w_ref[...], preferred_element_type=F32).astype(BF16)

    @pl.when(pl.program_id(1) == 0)
    def _():
        dt_ref[...] = jnp.dot(h_ref[...], wdt_ref[...], preferred_element_type=F32)


def _rest_proj(h2d, w_rest, w_dt):
    m = h2d.shape[0]
    return pl.pallas_call(
        _rest_kernel,
        grid=(m // TM, REST_WIDTH // REST_TN),
        in_specs=[pl.BlockSpec((TM, D_MODEL), lambda i, j: (i, 0)),
                  pl.BlockSpec((D_MODEL, REST_TN), lambda i, j: (0, j)),
                  pl.BlockSpec((D_MODEL, 2 * SSD_HEADS), lambda i, j: (0, 0))],
        out_specs=[pl.BlockSpec((TM, REST_TN), lambda i, j: (i, j)),
                   pl.BlockSpec((TM, 2 * SSD_HEADS), lambda i, j: (i, 0))],
        out_shape=[jax.ShapeDtypeStruct((m, REST_WIDTH), BF16),
                   jax.ShapeDtypeStruct((m, 2 * SSD_HEADS), F32)],
        compiler_params=_cparams(("parallel", "arbitrary")),
        name="rest_proj",
    )(h2d, w_rest, w_dt)


ATTN_TQ = 128
ATTN_RADIUS = 64


def _attn_kernel(qkv_ref, o_ref, l_ref, *, n, dilation):
    tq = ATTN_TQ
    nq = n // tq
    win = min(n, tq + 2 * ATTN_RADIUS)
    scale = HEAD_DIM ** -0.5

    def body(it, carry):
        r = it // nq
        i0 = pl.multiple_of((it % nq) * tq, tq)
        start = pl.multiple_of(jnp.clip(i0 - ATTN_RADIUS, 0, n - win), ATTN_RADIUS)
        rows = i0 + lax.broadcasted_iota(jnp.int32, (tq, win), 0)
        cols = start + lax.broadcasted_iota(jnp.int32, (tq, win), 1)
        valid = jnp.abs(cols - rows) <= ATTN_RADIUS
        q = qkv_ref[0, r, pl.ds(i0, tq), 0:HEAD_DIM]
        k = qkv_ref[0, r, pl.ds(start, win), HEAD_DIM:2 * HEAD_DIM]
        v = qkv_ref[0, r, pl.ds(start, win), 2 * HEAD_DIM:3 * HEAD_DIM]
        sc = lax.dot_general(q, k, (((1,), (1,)), ((), ())), preferred_element_type=F32) * scale
        sc = jnp.where(valid, sc, NEG_INF)
        mx = jnp.max(sc, axis=-1, keepdims=True)
        p = jnp.exp(sc - mx)
        den = jnp.sum(p, axis=-1, keepdims=True)
        o = jnp.dot(p.astype(BF16), v, preferred_element_type=F32) / den
        lse = jnp.broadcast_to(mx + jnp.log(den), (tq, HEAD_DIM))
        if dilation == 1:
            dst = pl.ds(i0, tq)
        else:
            dst = pl.ds(i0 * dilation + r, tq, stride=dilation)
        o_ref[0, dst, :] = o
        l_ref[0, dst, :] = lse
        return carry

    lax.fori_loop(0, dilation * nq, body, 0)


def _attention_group(qkv_g, dilation):
    bsz, _, n, _ = qkv_g.shape
    s = n * dilation
    ospec = pl.BlockSpec((1, s, HEAD_DIM), lambda b, hh: (b, 0, hh))
    return pl.pallas_call(
        functools.partial(_attn_kernel, n=n, dilation=dilation),
        grid=(bsz, HEADS_PER_GROUP),
        in_specs=[pl.BlockSpec((1, dilation, n, 3 * HEAD_DIM), lambda b, hh: (b, 0, 0, hh))],
        out_specs=[ospec, ospec],
        out_shape=[jax.ShapeDtypeStruct((bsz, s, ATTN_OUT_WIDTH), F32)] * 2,
        compiler_params=_cparams(("parallel", "parallel")),
        name=f"attn_d{dilation}",
    )(qkv_g)


def _ssd_kernel(*refs, reverse, final):
    L, P, N, R = SSD_CHUNK, SSD_HEAD_DIM, SSD_STATE, SSD_HEADS_PER_GROUP
    if final:
        (z_ref, xc_ref, dt_ref, dtb_ref, alog_ref, dskip_ref, nw_ref, sb_ref, y_ref,
         state_ref, xt_ref, bm_ref, dec_ref, tot_ref, cm_ref, at_ref, ar_ref, dtt_ref, yt_ref) = refs
    else:
        (xs_ref, xsp_ref, xsn_ref, bc_ref, bcp_ref, bcn_ref, dt_ref, cwx_ref, cbx_ref, cwb_ref, cbb_ref,
         dtb_ref, alog_ref, sout_ref, xc_ref,
         state_ref, xt_ref, bm_ref, dec_ref, tot_ref, ext_ref) = refs
    c = pl.program_id(1)
    nc = pl.num_programs(1)
    cc = nc - 1 - c if reverse else c
    dir_off = SSD_HEADS if reverse else 0
    gw = R * P

    @pl.when(c == 0)
    def _():
        state_ref[...] = jnp.zeros_like(state_ref)

    if not final:
        keep_prev = jnp.where(cc > 0, 1.0, 0.0).astype(F32)
        keep_next = jnp.where(cc < nc - 1, 1.0, 0.0).astype(F32)

        def conv_silu(main_ref, prev_ref, next_ref, w_ref, b_ref, col0, width):
            sl = slice(col0, col0 + width)
            ext_ref[0:HALO, 0:width] = prev_ref[0, :, sl].astype(F32) * keep_prev
            ext_ref[HALO:HALO + L, 0:width] = main_ref[0, :, sl].astype(F32)
            ext_ref[HALO + L:HALO + L + HALO, 0:width] = next_ref[0, :, sl].astype(F32) * keep_next
            acc = ext_ref[pl.ds(HALO - SSD_CONV // 2, L), 0:width] * w_ref[0:1, sl]
            for j in range(1, SSD_CONV):
                acc = acc + ext_ref[pl.ds(HALO - SSD_CONV // 2 + j, L), 0:width] * w_ref[j:j + 1, sl]
            acc = acc + b_ref[:, sl]
            return _silu(acc)

        for g in range(SSD_GROUPS):
            xg = conv_silu(xs_ref, xsp_ref, xsn_ref, cwx_ref, cbx_ref, g * gw, gw)
            xc_ref[0, :, g * gw:(g + 1) * gw] = xg.astype(BF16)
            for q in range(gw // LANES):
                blk = g * (gw // LANES) + q
                xt_ref[blk * LANES:(blk + 1) * LANES, :] = xg[:, q * LANES:(q + 1) * LANES].T
        for q in range(SSD_BC // gw):
            bcg = conv_silu(bc_ref, bcp_ref, bcn_ref, cwb_ref, cbb_ref, q * gw, gw).astype(BF16)
            xc_ref[0, :, SSD_INNER + q * gw:SSD_INNER + (q + 1) * gw] = bcg
            for t in range(gw // N):
                gidx = q * (gw // N) + t
                if gidx < SSD_GROUPS:
                    bm_ref[gidx] = bcg[:, t * N:(t + 1) * N]
    else:
        for blk in range(SSD_INNER // LANES):
            sl = slice(blk * LANES, (blk + 1) * LANES)
            xt_ref[sl, :] = xc_ref[0, :, sl].astype(F32).T
        for g in range(SSD_GROUPS):
            bm_ref[g] = xc_ref[0, :, SSD_INNER + g * N:SSD_INNER + (g + 1) * N]
            cm_ref[g] = xc_ref[0, :, SSD_INNER + (SSD_GROUPS + g) * N:SSD_INNER + (SSD_GROUPS + g + 1) * N].astype(F32)

    dt = jax.nn.softplus(dt_ref[0] + dtb_ref[...])
    dta = dt * (-jnp.exp(alog_ref[...]))
    row = lax.broadcasted_iota(jnp.int32, (L, L), 0)
    col = lax.broadcasted_iota(jnp.int32, (L, L), 1)
    cum_f = jnp.dot((col <= row).astype(F32), dta, preferred_element_type=F32, precision=lax.Precision.HIGHEST)
    cum_b = jnp.dot((col >= row).astype(F32), dta, preferred_element_type=F32, precision=lax.Precision.HIGHEST)
    lane = lax.broadcasted_iota(jnp.int32, (L, 2 * SSD_HEADS), 1)
    acum_t = jnp.where(lane < SSD_HEADS, cum_f, cum_b).T
    dt_t = dt.T
    hrow = lax.broadcasted_iota(jnp.int32, (2 * SSD_HEADS, 1), 0)
    tot = jnp.where(hrow < SSD_HEADS, acum_t[:, L - 1:L], acum_t[:, 0:1])
    dec_ref[...] = jnp.exp(tot - acum_t) * dt_t
    tot_ref[...] = jnp.broadcast_to(jnp.exp(tot), (2 * SSD_HEADS, N))
    if final:
        at_ref[...] = acum_t * LOG2E
        ar_ref[...] = (acum_t - jnp.log(dt_t)) * LOG2E
        dtt_ref[...] = dt_t
    else:
        sout_ref[0, 0] = state_ref[...].astype(BF16)

    def bcast_row(ref, r, rows):
        return jnp.broadcast_to(ref[pl.ds(r, 1), :], (rows, ref.shape[1]))

    def group_body(g, carry):
        b_g = bm_ref[g]
        if final:
            c_g = cm_ref[g]
            cb = lax.dot_general(c_g.astype(BF16), b_g, (((1,), (1,)), ((), ())), preferred_element_type=F32)
        for r in range(R):
            h = g * R + r
            hrows = pl.ds(pl.multiple_of(h * P, P), P)
            x_t = xt_ref[hrows, :]
            s_old = state_ref[hrows, :]
            if final:
                hf = h
                hb = h + SSD_HEADS
                acol_f = bcast_row(at_ref, hf, L).T
                acol_b = bcast_row(at_ref, hb, L).T
                arg = jnp.where(row >= col, acol_f - bcast_row(ar_ref, hf, L), acol_b - bcast_row(ar_ref, hb, L))
                wd = jnp.exp2(arg) + jnp.where(row == col, bcast_row(dtt_ref, hb, L), 0.0)
                w = (cb * wd).astype(BF16)
                ec_f = (c_g * jnp.exp2(acol_f)).astype(BF16)
                ec_b = (c_g * jnp.exp2(acol_b)).astype(BF16)
                nt = (((1,), (1,)), ((), ()))
                y_t = lax.dot_general(x_t.astype(BF16), w, nt, preferred_element_type=F32)
                y_t = y_t + lax.dot_general(s_old.astype(BF16), ec_f, nt, preferred_element_type=F32)
                y_t = y_t + lax.dot_general(sb_ref[0, 0, hrows, :], ec_b, nt, preferred_element_type=F32)
                yt_ref[hrows, :] = y_t
            hd = h + dir_off
            xsc = (x_t * bcast_row(dec_ref, hd, P)).astype(BF16)
            state_ref[hrows, :] = s_old * bcast_row(tot_ref, hd, P) + jnp.dot(xsc, b_g, preferred_element_type=F32)
        return carry

    lax.fori_loop(0, SSD_GROUPS, group_body, 0)

    if final:
        for g in range(SSD_GROUPS):
            ys = []
            ssq = None
            for q in range(gw // LANES):
                blk = g * (gw // LANES) + q
                sl = slice(blk * LANES, (blk + 1) * LANES)
                yb = yt_ref[sl, :].T + xc_ref[0, :, sl].astype(F32) * dskip_ref[:, sl]
                yb = yb * _silu(z_ref[0, :, sl].astype(F32))
                ys.append(yb)
                part = jnp.sum(yb * yb, axis=-1, keepdims=True)
                ssq = part if ssq is None else ssq + part
            inv = lax.rsqrt(ssq * (1.0 / gw) + EPS)
            for q, yb in enumerate(ys):
                blk = g * (gw // LANES) + q
                sl = slice(blk * LANES, (blk + 1) * LANES)
                y_ref[0, :, sl] = (yb * inv * nw_ref[:, sl]).astype(BF16)


def _ssd(rest, dt_raw, conv_w, conv_b, dt_bias, a_log, d_skip, norm_w):
    bsz, s, _ = rest.shape
    L = SSD_CHUNK
    nc = s // L
    hb = L // HALO
    n_halo = s // HALO
    cwx, cwb = conv_w[:, :SSD_INNER], conv_w[:, SSD_INNER:]
    cbx, cbb = conv_b[:, :SSD_INNER], conv_b[:, SSD_INNER:]

    rev = lambda c: nc - 1 - c
    prev = lambda col: (lambda b, c: (b, jnp.maximum(rev(c) * hb - 1, 0), col))
    nxt = lambda col: (lambda b, c: (b, jnp.minimum((rev(c) + 1) * hb, n_halo - 1), col))
    const = lambda b, c: (0, 0)
    xs_col, bc_col = REST_XS // SSD_INNER, REST_BC // SSD_BC
    xbc_w = SSD_INNER + SSD_BC
    common_scratch = [
        pltpu.VMEM((SSD_INNER, SSD_STATE), F32),
        pltpu.VMEM((SSD_INNER, L), F32),
        pltpu.VMEM((SSD_GROUPS, L, SSD_STATE), BF16),
        pltpu.VMEM((2 * SSD_HEADS, L), F32),
        pltpu.VMEM((2 * SSD_HEADS, SSD_STATE), F32),
    ]

    states_b, xc = pl.pallas_call(
        functools.partial(_ssd_kernel, reverse=True, final=False),
        grid=(bsz, nc),
        in_specs=[
            pl.BlockSpec((1, L, SSD_INNER), lambda b, c: (b, rev(c), xs_col)),
            pl.BlockSpec((1, HALO, SSD_INNER), prev(xs_col)),
            pl.BlockSpec((1, HALO, SSD_INNER), nxt(xs_col)),
            pl.BlockSpec((1, L, SSD_BC), lambda b, c: (b, rev(c), bc_col)),
            pl.BlockSpec((1, HALO, SSD_BC), prev(bc_col)),
            pl.BlockSpec((1, HALO, SSD_BC), nxt(bc_col)),
            pl.BlockSpec((1, L, 2 * SSD_HEADS), lambda b, c: (b, rev(c), 0)),
            pl.BlockSpec((SSD_CONV, SSD_INNER), const),
            pl.BlockSpec((1, SSD_INNER), const),
            pl.BlockSpec((SSD_CONV, SSD_BC), const),
            pl.BlockSpec((1, SSD_BC), const),
            pl.BlockSpec((1, 2 * SSD_HEADS), const),
            pl.BlockSpec((1, 2 * SSD_HEADS), const),
        ],
        out_specs=[pl.BlockSpec((1, 1, SSD_INNER, SSD_STATE), lambda b, c: (b, rev(c), 0, 0)),
                   pl.BlockSpec((1, L, xbc_w), lambda b, c: (b, rev(c), 0))],
        out_shape=[jax.ShapeDtypeStruct((bsz, nc, SSD_INNER, SSD_STATE), BF16),
                   jax.ShapeDtypeStruct((bsz, s, xbc_w), BF16)],
        scratch_shapes=common_scratch + [
            pltpu.VMEM((L + 2 * HALO, SSD_HEADS_PER_GROUP * SSD_HEAD_DIM), F32),
        ],
        compiler_params=_cparams(("parallel", "arbitrary")),
        name="ssd_bwd_states",
    )(rest, rest, rest, rest, rest, rest, dt_raw, cwx, cbx, cwb, cbb, dt_bias, a_log)

    y = pl.pallas_call(
        functools.partial(_ssd_kernel, reverse=False, final=True),
        grid=(bsz, nc),
        in_specs=[
            pl.BlockSpec((1, L, SSD_INNER), lambda b, c: (b, c, REST_Z // SSD_INNER)),
            pl.BlockSpec((1, L, xbc_w), lambda b, c: (b, c, 0)),
            pl.BlockSpec((1, L, 2 * SSD_HEADS), lambda b, c: (b, c, 0)),
            pl.BlockSpec((1, 2 * SSD_HEADS), const),
            pl.BlockSpec((1, 2 * SSD_HEADS), const),
            pl.BlockSpec((1, SSD_INNER), const),
            pl.BlockSpec((1, SSD_INNER), const),
            pl.BlockSpec((1, 1, SSD_INNER, SSD_STATE), lambda b, c: (b, c, 0, 0)),
        ],
        out_specs=pl.BlockSpec((1, L, SSD_INNER), lambda b, c: (b, c, 0)),
        out_shape=jax.ShapeDtypeStruct((bsz, s, SSD_INNER), BF16),
        scratch_shapes=common_scratch + [
            pltpu.VMEM((SSD_GROUPS, L, SSD_STATE), F32),
            pltpu.VMEM((2 * SSD_HEADS, L), F32),
            pltpu.VMEM((2 * SSD_HEADS, L), F32),
            pltpu.VMEM((2 * SSD_HEADS, L), F32),
            pltpu.VMEM((SSD_INNER, L), F32),
        ],
        compiler_params=_cparams(("parallel", "arbitrary")),
        name="ssd_fwd",
    )(rest, xc, dt_raw, dt_bias, a_log, d_skip, norm_w, states_b)
    return y


MIX_TN = 512


def _mix_kernel(o0, o1, o2, l0, l1, l2, y_ref, ga_ref, gs_ref, wa_ref, ws_ref, m_ref, attn_ref):
    @pl.when(pl.program_id(1) == 0)
    def _():
        ls = [l0[...], l1[...], l2[...]]
        mx = jnp.maximum(jnp.maximum(ls[0], ls[1]), ls[2])
        es = [jnp.exp(l - mx) for l in ls]
        den = es[0] + es[1] + es[2]
        num = es[0] * o0[...].astype(F32) + es[1] * o1[...].astype(F32) + es[2] * o2[...].astype(F32)
        attn_ref[...] = (num / den).astype(BF16)

    a_br = jnp.dot(attn_ref[...], wa_ref[...], preferred_element_type=F32)
    s_br = jnp.dot(y_ref[...], ws_ref[...], preferred_element_type=F32)
    merged = jax.nn.sigmoid(ga_ref[...].astype(F32)) * a_br + jax.nn.sigmoid(gs_ref[...].astype(F32)) * s_br
    m_ref[...] = merged.astype(BF16)


def _mix(attn_outs, attn_lses, y2d, rest2d, w_attn_proj, w_ssd_proj):
    m = y2d.shape[0]
    tm = 512
    aspec = pl.BlockSpec((tm, ATTN_OUT_WIDTH), lambda i, j: (i, 0))
    ga0 = REST_GA // MIX_TN
    gs0 = REST_GS // MIX_TN
    return pl.pallas_call(
        _mix_kernel,
        grid=(m // tm, D_MODEL // MIX_TN),
        in_specs=[aspec] * 6 + [
            pl.BlockSpec((tm, SSD_INNER), lambda i, j: (i, 0)),
            pl.BlockSpec((tm, MIX_TN), lambda i, j: (i, ga0 + j)),
            pl.BlockSpec((tm, MIX_TN), lambda i, j: (i, gs0 + j)),
            pl.BlockSpec((ATTN_OUT_WIDTH, MIX_TN), lambda i, j: (0, j)),
            pl.BlockSpec((SSD_INNER, MIX_TN), lambda i, j: (0, j)),
        ],
        out_specs=pl.BlockSpec((tm, MIX_TN), lambda i, j: (i, j)),
        out_shape=jax.ShapeDtypeStruct((m, D_MODEL), BF16),
        scratch_shapes=[pltpu.VMEM((tm, ATTN_OUT_WIDTH), BF16)],
        compiler_params=_cparams(("parallel", "arbitrary")),
        name="mix",
    )(*attn_outs, *attn_lses, y2d, rest2d, rest2d, w_attn_proj, w_ssd_proj)


def _out_kernel(m_ref, w_ref, x_ref, mod_ref, nw_ref, x1_ref, h2_ref):
    mix = jnp.dot(m_ref[0], w_ref[...], preferred_element_type=F32)
    x1 = x_ref[0] + mod_ref[0, 2:3, :] * mix
    x1_ref[0] = x1
    h2_ref[0] = _rms_mod(x1, nw_ref[...], mod_ref[0, 4:5, :], mod_ref[0, 3:4, :]).astype(BF16)


def _out_proj(merged, w_out, x, mod, norm_w):
    bsz, s, _ = x.shape
    ts = 256
    tok = pl.BlockSpec((1, ts, D_MODEL), lambda b, i: (b, i, 0))
    return pl.pallas_call(
        _out_kernel,
        grid=(bsz, s // ts),
        in_specs=[tok,
                  pl.BlockSpec((D_MODEL, D_MODEL), lambda b, i: (0, 0)),
                  tok,
                  pl.BlockSpec((1, 6, D_MODEL), lambda b, i: (b, 0, 0)),
                  pl.BlockSpec((1, D_MODEL), lambda b, i: (0, 0))],
        out_specs=[tok, tok],
        out_shape=[jax.ShapeDtypeStruct((bsz, s, D_MODEL), F32),
                   jax.ShapeDtypeStruct((bsz, s, D_MODEL), BF16)],
        compiler_params=_cparams(("parallel", "parallel")),
        name="out_proj",
    )(merged, w_out, x, mod, norm_w.reshape(1, -1))


UP_TN = 1024


def _up_kernel(h_ref, w_ref, o_ref):
    o_ref[...] = jnp.dot(h_ref[...], w_ref[...], preferred_element_type=F32).astype(BF16)


def _up_proj(h2d, w_up):
    m = h2d.shape[0]
    return pl.pallas_call(
        _up_kernel,
        grid=(m // TM, 2 * D_FF // UP_TN),
        in_specs=[pl.BlockSpec((TM, D_MODEL), lambda i, j: (i, 0)),
                  pl.BlockSpec((D_MODEL, UP_TN), lambda i, j: (0, j))],
        out_specs=pl.BlockSpec((TM, UP_TN), lambda i, j: (i, j)),
        out_shape=jax.ShapeDtypeStruct((m, 2 * D_FF), BF16),
        compiler_params=_cparams(("parallel", "arbitrary")),
        name="ffn_up",
    )(h2d, w_up)


DOWN_TM = 512
DOWN_TK = 512


def _down_kernel(g_ref, gp_ref, gn_ref, v_ref, cw_ref, cb_ref, w_ref, x1_ref, mod_ref, nw_ref, o_ref,
                 acc_ref, ext_ref, *, tiles_per_seq):
    i = pl.program_id(0)
    k = pl.program_id(1)
    tm = DOWN_TM

    @pl.when(k == 0)
    def _():
        acc_ref[...] = jnp.zeros_like(acc_ref)

    keep_prev = jnp.where(i % tiles_per_seq > 0, 1.0, 0.0).astype(F32)
    keep_next = jnp.where(i % tiles_per_seq < tiles_per_seq - 1, 1.0, 0.0).astype(F32)
    ext_ref[0:HALO, :] = gp_ref[...].astype(F32) * keep_prev
    ext_ref[HALO:HALO + tm, :] = g_ref[...].astype(F32)
    ext_ref[HALO + tm:HALO + tm + HALO, :] = gn_ref[...].astype(F32) * keep_next
    gate = cb_ref[0]
    for j in range(FFN_CONV):
        gate = gate + ext_ref[pl.ds(HALO - FFN_CONV // 2 + j, tm), :] * cw_ref[0, j:j + 1, :]
    act = 0.5 * gate * (1.0 + lax.erf(gate * (2.0 ** -0.5))) * v_ref[...].astype(F32)
    acc_ref[...] += jnp.dot(act.astype(BF16), w_ref[...], preferred_element_type=F32)

    @pl.when(k == pl.num_programs(1) - 1)
    def _():
        x2 = x1_ref[...] + mod_ref[0, 5:6, :] * acc_ref[...]
        o_ref[...] = x2 * lax.rsqrt(jnp.mean(x2 * x2, axis=-1, keepdims=True) + EPS) * nw_ref[...]


def _down_proj(up2d, conv_w, conv_b, w_down, x1_2d, mod, norm_w, s):
    m = up2d.shape[0]
    tm, tk = DOWN_TM, DOWN_TK
    nk = D_FF // tk
    tiles_per_seq = s // tm
    hb = tm // HALO
    n_halo = m // HALO
    cw = conv_w.reshape(FFN_CONV, nk, tk).transpose(1, 0, 2)
    cb = conv_b.reshape(nk, 1, tk)
    return pl.pallas_call(
        functools.partial(_down_kernel, tiles_per_seq=tiles_per_seq),
        grid=(m // tm, nk),
        in_specs=[pl.BlockSpec((tm, tk), lambda i, k: (i, k)),
                  pl.BlockSpec((HALO, tk), lambda i, k: (jnp.maximum(i * hb - 1, 0), k)),
                  pl.BlockSpec((HALO, tk), lambda i, k: (jnp.minimum((i + 1) * hb, n_halo - 1), k)),
                  pl.BlockSpec((tm, tk), lambda i, k: (i, nk + k)),
                  pl.BlockSpec((1, FFN_CONV, tk), lambda i, k: (k, 0, 0)),
                  pl.BlockSpec((1, 1, tk), lambda i, k: (k, 0, 0)),
                  pl.BlockSpec((tk, D_MODEL), lambda i, k: (k, 0)),
                  pl.BlockSpec((tm, D_MODEL), lambda i, k: (i, 0)),
                  pl.BlockSpec((1, 6, D_MODEL), lambda i, k: (i // tiles_per_seq, 0, 0)),
                  pl.BlockSpec((1, D_MODEL), lambda i, k: (0, 0))],
        out_specs=pl.BlockSpec((tm, D_MODEL), lambda i, k: (i, 0)),
        out_shape=jax.ShapeDtypeStruct((m, D_MODEL), F32),
        scratch_shapes=[pltpu.VMEM((tm, D_MODEL), F32), pltpu.VMEM((tm + 2 * HALO, tk), F32)],
        compiler_params=_cparams(("parallel", "arbitrary")),
        name="ffn_down",
    )(up2d, up2d, up2d, up2d, cw, cb, w_down, x1_2d, mod, norm_w.reshape(1, -1))


def _rotary_tables(s):
    pos = jnp.arange(s, dtype=F32)
    inv_freq = ROPE_THETA ** (-jnp.arange(0, HEAD_DIM, 2, dtype=F32) / HEAD_DIM)
    ang = pos[:, None] * inv_freq[None, :]
    cos, sin = jnp.cos(ang), jnp.sin(ang)
    return jnp.concatenate([cos, cos], axis=-1), jnp.concatenate([-sin, sin], axis=-1)


def _prep_weights(w_in, w_attn_proj, w_ssd_proj, w_out, w_up, w_down):
    o = np.cumsum((0, ATTN_WIDTH, ATTN_WIDTH, ATTN_WIDTH, SSD_INNER, SSD_INNER + SSD_BC, 2 * SSD_HEADS,
                   D_MODEL, D_MODEL))
    wq, wk, wv = (w_in[:, o[t]:o[t + 1]].reshape(D_MODEL, N_ATTN_HEADS, HEAD_DIM) for t in range(3))
    w_qkv = jnp.stack([wq, wk, wv], axis=2).reshape(D_MODEL, QKV_WIDTH)
    w_rest = jnp.concatenate([w_in[:, o[3]:o[5]], w_in[:, o[6]:o[8]]], axis=1)
    w_dt = w_in[:, o[5]:o[6]]
    cast = lambda w: w.astype(BF16)
    return (cast(w_qkv), cast(w_rest), cast(w_dt), cast(w_attn_proj), cast(w_ssd_proj), cast(w_out),
            cast(w_up), cast(w_down))


def _trunk(x, mod, wts, small):
    bsz, s, _ = x.shape
    m = bsz * s
    w_qkv, w_rest, w_dt, w_attn_proj, w_ssd_proj, w_out, w_up, w_down = wts
    (norm_mix_w, ssd_conv_w, ssd_conv_b, dt_bias, a_log, d_skip, ssd_norm_w, norm_ffn_w, ffn_conv_w, ffn_conv_b,
     norm_f_w) = small
    h = _norm_mod(x, mod, norm_mix_w)
    h2d = h.reshape(m, D_MODEL)
    cos_t, sin_t = _rotary_tables(s)
    rest2d, dt_raw = _rest_proj(h2d, w_rest, w_dt)
    outs, lses = [], []
    for gi, (_, dilation) in enumerate(ATTN_GROUPS):
        o, l = _attention_group(_qkv_proj(h, w_qkv, cos_t, sin_t, gi, dilation), dilation)
        outs.append(o.reshape(m, ATTN_OUT_WIDTH))
        lses.append(l.reshape(m, ATTN_OUT_WIDTH))
    y = _ssd(rest2d.reshape(bsz, s, REST_WIDTH), dt_raw.reshape(bsz, s, 2 * SSD_HEADS), ssd_conv_w, ssd_conv_b,
             dt_bias, a_log, d_skip, ssd_norm_w)
    merged = _mix(outs, lses, y.reshape(m, SSD_INNER), rest2d, w_attn_proj, w_ssd_proj)
    x1, h2 = _out_proj(merged.reshape(bsz, s, D_MODEL), w_out, x, mod, norm_ffn_w)
    up = _up_proj(h2.reshape(m, D_MODEL), w_up)
    out = _down_proj(up, ffn_conv_w, ffn_conv_b, w_down, x1.reshape(m, D_MODEL), mod, norm_f_w, s)
    return out.reshape(bsz, s, D_MODEL)


def kernel(x_prompt, x_sample, c_prompt, c_sample, w_ada, b_ada, norm_mix_w, w_in, ssd_conv_w, ssd_conv_b,
           dt_bias_fwd, dt_bias_bwd, a_log_fwd, a_log_bwd, ssd_d, ssd_norm_w, w_attn_proj, w_ssd_proj, w_out,
           norm_ffn_w, w_up, ffn_conv_w, ffn_conv_b, w_down, norm_f_w):
    wts = _prep_weights(w_in[0], w_attn_proj[0], w_ssd_proj[0], w_out[0], w_up[0], w_down[0])
    small = (norm_mix_w[0], ssd_conv_w[0], ssd_conv_b[0].reshape(1, -1),
             jnp.concatenate([dt_bias_fwd[0], dt_bias_bwd[0]]).reshape(1, -1),
             jnp.concatenate([a_log_fwd[0], a_log_bwd[0]]).reshape(1, -1),
             jnp.repeat(ssd_d[0], SSD_HEAD_DIM).reshape(1, -1), ssd_norm_w[0].reshape(1, -1),
             norm_ffn_w[0], ffn_conv_w[0], ffn_conv_b[0], norm_f_w)
    nb_p = c_prompt.shape[0]
    nb_s = c_sample.shape[0]
    pad = (-(nb_p + nb_s)) % 8
    c_all = jnp.concatenate([c_prompt, c_sample, jnp.zeros((pad, D_MODEL), F32)], axis=0)
    mod_all = _modulation(c_all, w_ada[0], b_ada[0]).reshape(-1, 6, D_MODEL)
    y_prompt = _trunk(x_prompt, mod_all[:nb_p], wts, small)
    y_sample = _trunk(x_sample, mod_all[nb_p:nb_p + nb_s], wts, small)
    return (y_prompt, y_sample)
```

```python
import functools

import numpy as np
import jax
import jax.numpy as jnp
from jax import lax
from jax.experimental import pallas as pl
from jax.experimental.pallas import tpu as pltpu

F32 = jnp.float32
BF16 = jnp.bfloat16

D_MODEL = 2048
HEAD_DIM = 128
ATTN_GROUPS = ((128, 1), (512, 4), (2048, 16))
HEADS_PER_GROUP = 4
N_ATTN_HEADS = HEADS_PER_GROUP * len(ATTN_GROUPS)
ATTN_WIDTH = N_ATTN_HEADS * HEAD_DIM
ATTN_OUT_WIDTH = HEADS_PER_GROUP * HEAD_DIM
ROPE_THETA = 10000.0
SSD_INNER = 2 * D_MODEL
SSD_HEAD_DIM = 64
SSD_HEADS = SSD_INNER // SSD_HEAD_DIM
SSD_GROUPS = 8
SSD_HEADS_PER_GROUP = SSD_HEADS // SSD_GROUPS
SSD_STATE = 128
SSD_CONV = 5
SSD_CHUNK = 128
SSD_BC = 2 * SSD_GROUPS * SSD_STATE
D_FF = 5632
FFN_CONV = 3
EPS = 1e-6
NEG_INF = -1e30
LOG2E = 1.4426950408889634

QKV_WIDTH = 3 * ATTN_WIDTH
REST_WIDTH = 2 * SSD_INNER + SSD_BC + 2 * D_MODEL
REST_Z, REST_XS, REST_BC, REST_GA, REST_GS = 0, SSD_INNER, 2 * SSD_INNER, 2 * SSD_INNER + SSD_BC, 2 * SSD_INNER + SSD_BC + D_MODEL

LANES = 128
SUBLANES = 8
HALO = 16
VMEM_LIMIT = 56 * 1024 * 1024
TM = 1024


def _cparams(sem):
    return pltpu.CompilerParams(dimension_semantics=sem, vmem_limit_bytes=VMEM_LIMIT)


def _silu(x):
    h = 0.5 * x
    return h + h * jnp.tanh(h)


def _shifted_rows(ext, shift, rows):
    if shift == 0:
        return ext[SUBLANES:SUBLANES + rows]
    return pltpu.roll(ext, (-shift) % ext.shape[0], 0)[SUBLANES:SUBLANES + rows]


def _with_halo(main_ref_val, prev_blk, next_blk, keep_prev, keep_next):
    prev = prev_blk.astype(F32)[HALO - SUBLANES:HALO] * keep_prev
    nxt = next_blk.astype(F32)[0:SUBLANES] * keep_next
    return jnp.concatenate([prev, main_ref_val.astype(F32), nxt], axis=0)


def _mod_kernel(c_ref, w_ref, b_ref, o_ref):
    c = c_ref[...]
    o_ref[...] = jnp.dot(c * jax.nn.sigmoid(c), w_ref[...], preferred_element_type=F32,
                         precision=lax.Precision.HIGHEST) + b_ref[...]


def _modulation(c_all, w_ada, b_ada):
    rows = c_all.shape[0]
    tn = 1024
    return pl.pallas_call(
        _mod_kernel,
        grid=(6 * D_MODEL // tn,),
        in_specs=[pl.BlockSpec((rows, D_MODEL), lambda j: (0, 0)),
                  pl.BlockSpec((D_MODEL, tn), lambda j: (0, j)),
                  pl.BlockSpec((1, tn), lambda j: (0, j))],
        out_specs=pl.BlockSpec((rows, tn), lambda j: (0, j)),
        out_shape=jax.ShapeDtypeStruct((rows, 6 * D_MODEL), F32),
        compiler_params=_cparams(("arbitrary",)),
        name="mod",
    )(c_all, w_ada, b_ada.reshape(1, -1))


def _rms_mod(x, w, scale, shift):
    y = x * lax.rsqrt(jnp.mean(x * x, axis=-1, keepdims=True) + EPS) * w
    return y * (1.0 + scale) + shift


def _norm_kernel(x_ref, mod_ref, w_ref, h_ref):
    h_ref[0] = _rms_mod(x_ref[0], w_ref[...], mod_ref[0, 1:2, :], mod_ref[0, 0:1, :]).astype(BF16)


def _norm_mod(x, mod, w):
    bsz, s, _ = x.shape
    ts = 512
    return pl.pallas_call(
        _norm_kernel,
        grid=(bsz, s // ts),
        in_specs=[pl.BlockSpec((1, ts, D_MODEL), lambda b, i: (b, i, 0)),
                  pl.BlockSpec((1, 6, D_MODEL), lambda b, i: (b, 0, 0)),
                  pl.BlockSpec((1, D_MODEL), lambda b, i: (0, 0))],
        out_specs=pl.BlockSpec((1, ts, D_MODEL), lambda b, i: (b, i, 0)),
        out_shape=jax.ShapeDtypeStruct((bsz, s, D_MODEL), BF16),
        compiler_params=_cparams(("parallel", "parallel")),
        name="norm_mix",
    )(x, mod, w.reshape(1, -1))


QKV_TN = 6 * HEAD_DIM
QKV_GROUP_WIDTH = 3 * ATTN_OUT_WIDTH


def _qkv_kernel(h_ref, w_ref, cos_ref, sin_ref, o_ref, *scratch, dilation):
    acc = jnp.dot(h_ref[0], w_ref[...], preferred_element_type=F32)
    cos = cos_ref[...]
    sin = sin_ref[...]
    tiles = QKV_TN // HEAD_DIM
    for t in range(tiles):
        sl = slice(t * HEAD_DIM, (t + 1) * HEAD_DIM)
        a = acc[:, sl]
        if t % 3 != 2:
            a = a * cos + pltpu.roll(a, HEAD_DIM // 2, 1) * sin
        if dilation == 1:
            o_ref[0, 0, :, sl] = a.astype(BF16)
        else:
            scratch[0][t] = a
    if dilation > 1:
        rows = TM // dilation
        for r in range(dilation):
            for t in range(tiles):
                sl = slice(t * HEAD_DIM, (t + 1) * HEAD_DIM)
                o_ref[0, r, :, sl] = scratch[0][t, pl.ds(r, rows, stride=dilation), :].astype(BF16)


def _qkv_proj(h, w_qkv, cos_t, sin_t, gi, dilation):
    bsz, s, _ = h.shape
    col_tiles = QKV_GROUP_WIDTH // QKV_TN
    rows = TM // dilation
    return pl.pallas_call(
        functools.partial(_qkv_kernel, dilation=dilation),
        grid=(bsz, s // TM, col_tiles),
        in_specs=[pl.BlockSpec((1, TM, D_MODEL), lambda b, i, j: (b, i, 0)),
                  pl.BlockSpec((D_MODEL, QKV_TN), lambda b, i, j: (0, gi * col_tiles + j)),
                  pl.BlockSpec((TM, HEAD_DIM), lambda b, i, j: (i, 0)),
                  pl.BlockSpec((TM, HEAD_DIM), lambda b, i, j: (i, 0))],
        out_specs=pl.BlockSpec((1, dilation, rows, QKV_TN), lambda b, i, j: (b, 0, i, j)),
        out_shape=jax.ShapeDtypeStruct((bsz, dilation, s // dilation, QKV_GROUP_WIDTH), BF16),
        scratch_shapes=[pltpu.VMEM((QKV_TN // HEAD_DIM, TM, HEAD_DIM), F32)] if dilation > 1 else [],
        compiler_params=_cparams(("parallel", "parallel", "arbitrary")),
        name=f"qkv_proj_d{dilation}",
    )(h, w_qkv, cos_t, sin_t)


REST_TN = 1024


def _rest_kernel(h_ref, w_ref, wdt_ref, o_ref, dt_ref):
    o_ref[...] = jnp.dot(h_ref[...], w_ref[...], preferred_element_type=F32).astype(BF16)

    @pl.when(pl.program_id(1) == 0)
    def _():
        dt_ref[...] = jnp.dot(h_ref[...], wdt_ref[...], preferred_element_type=F32)


def _rest_proj(h2d, w_rest, w_dt):
    m = h2d.shape[0]
    return pl.pallas_call(
        _rest_kernel,
        grid=(m // TM, REST_WIDTH // REST_TN),
        in_specs=[pl.BlockSpec((TM, D_MODEL), lambda i, j: (i, 0)),
                  pl.BlockSpec((D_MODEL, REST_TN), lambda i, j: (0, j)),
                  pl.BlockSpec((D_MODEL, 2 * SSD_HEADS), lambda i, j: (0, 0))],
        out_specs=[pl.BlockSpec((TM, REST_TN), lambda i, j: (i, j)),
                   pl.BlockSpec((TM, 2 * SSD_HEADS), lambda i, j: (i, 0))],
        out_shape=[jax.ShapeDtypeStruct((m, REST_WIDTH), BF16),
                   jax.ShapeDtypeStruct((m, 2 * SSD_HEADS), F32)],
        compiler_params=_cparams(("parallel", "arbitrary")),
        name="rest_proj",
    )(h2d, w_rest, w_dt)


ATTN_TQ = 128
ATTN_RADIUS = 64
ATTN_UNROLL = 4


def _attn_kernel(qkv_ref, o_ref, l_ref, *, n, dilation):
    tq = ATTN_TQ
    nq = n // tq
    win = min(n, tq + 2 * ATTN_RADIUS)
    scale = HEAD_DIM ** -0.5

    def block(it):
        r = it // nq
        i0 = pl.multiple_of((it % nq) * tq, tq)
        start = pl.multiple_of(jnp.clip(i0 - ATTN_RADIUS, 0, n - win), ATTN_RADIUS)
        rows = i0 + lax.broadcasted_iota(jnp.int32, (tq, win), 0)
        cols = start + lax.broadcasted_iota(jnp.int32, (tq, win), 1)
        valid = jnp.abs(cols - rows) <= ATTN_RADIUS
        q = qkv_ref[0, r, pl.ds(i0, tq), 0:HEAD_DIM]
        k = qkv_ref[0, r, pl.ds(start, win), HEAD_DIM:2 * HEAD_DIM]
        v = qkv_ref[0, r, pl.ds(start, win), 2 * HEAD_DIM:3 * HEAD_DIM]
        sc = lax.dot_general(q, k, (((1,), (1,)), ((), ())), preferred_element_type=F32) * scale
        sc = jnp.where(valid, sc, NEG_INF)
        mx = jnp.max(sc, axis=-1, keepdims=True)
        p = jnp.exp(sc - mx)
        den = jnp.sum(p, axis=-1, keepdims=True)
        o = jnp.dot(p.astype(BF16), v, preferred_element_type=F32) / den
        lse = jnp.broadcast_to(mx + jnp.log(den), (tq, HEAD_DIM))
        if dilation == 1:
            dst = pl.ds(i0, tq)
        else:
            dst = pl.ds(i0 * dilation + r, tq, stride=dilation)
        o_ref[0, dst, :] = o
        l_ref[0, dst, :] = lse

    def body(trip, carry):
        for u in range(ATTN_UNROLL):
            block(trip * ATTN_UNROLL + u)
        return carry

    lax.fori_loop(0, dilation * nq // ATTN_UNROLL, body, 0)


def _attention_group(qkv_g, dilation):
    bsz, _, n, _ = qkv_g.shape
    s = n * dilation
    assert (dilation * (n // ATTN_TQ)) % ATTN_UNROLL == 0
    ospec = pl.BlockSpec((1, s, HEAD_DIM), lambda b, hh: (b, 0, hh))
    return pl.pallas_call(
        functools.partial(_attn_kernel, n=n, dilation=dilation),
        grid=(bsz, HEADS_PER_GROUP),
        in_specs=[pl.BlockSpec((1, dilation, n, 3 * HEAD_DIM), lambda b, hh: (b, 0, 0, hh))],
        out_specs=[ospec, ospec],
        out_shape=[jax.ShapeDtypeStruct((bsz, s, ATTN_OUT_WIDTH), F32)] * 2,
        compiler_params=_cparams(("parallel", "parallel")),
        name=f"attn_d{dilation}",
    )(qkv_g)


def _ssd_decay_terms(dt_ref, dtb_ref, alog_ref):
    L = SSD_CHUNK
    dt = jax.nn.softplus(dt_ref[0] + dtb_ref[...])
    dta = dt * (-jnp.exp(alog_ref[...]))
    row = lax.broadcasted_iota(jnp.int32, (L, L), 0)
    col = lax.broadcasted_iota(jnp.int32, (L, L), 1)
    cum_f = jnp.dot((col <= row).astype(F32), dta, preferred_element_type=F32, precision=lax.Precision.HIGHEST)
    cum_b = jnp.dot((col >= row).astype(F32), dta, preferred_element_type=F32, precision=lax.Precision.HIGHEST)
    lane = lax.broadcasted_iota(jnp.int32, (L, 2 * SSD_HEADS), 1)
    acum_t = jnp.where(lane < SSD_HEADS, cum_f, cum_b).T
    dt_t = dt.T
    hrow = lax.broadcasted_iota(jnp.int32, (2 * SSD_HEADS, 1), 0)
    tot = jnp.where(hrow < SSD_HEADS, acum_t[:, L - 1:L], acum_t[:, 0:1])
    return acum_t, dt_t, tot


def _bcast_row(ref, r, rows):
    return jnp.broadcast_to(ref[pl.ds(r, 1), :], (rows, ref.shape[1]))


def _ssd_bwd_kernel(xs_ref, xsp_ref, xsn_ref, bc_ref, bcp_ref, bcn_ref, dt_ref, cwx_ref, cbx_ref, cwb_ref, cbb_ref,
                    dtb_ref, alog_ref, sout_ref, xt_out_ref, bc_out_ref, state_ref, xt_ref, dec_ref, tot_ref):
    L, P, N, R = SSD_CHUNK, SSD_HEAD_DIM, SSD_STATE, SSD_HEADS_PER_GROUP
    gw = R * P
    c = pl.program_id(1)
    nc = pl.num_programs(1)
    cc = nc - 1 - c

    @pl.when(c == 0)
    def _():
        state_ref[...] = jnp.zeros_like(state_ref)

    keep_prev = jnp.where(cc > 0, 1.0, 0.0).astype(F32)
    keep_next = jnp.where(cc < nc - 1, 1.0, 0.0).astype(F32)

    def conv_silu(main_ref, prev_ref, next_ref, w_ref, b_ref, col0):
        sl = slice(col0, col0 + gw)
        ext = _with_halo(main_ref[0, :, sl], prev_ref[0, :, sl], next_ref[0, :, sl], keep_prev, keep_next)
        acc = b_ref[:, sl] + ext[SUBLANES:SUBLANES + L] * w_ref[SSD_CONV // 2:SSD_CONV // 2 + 1, sl]
        for j in range(SSD_CONV):
            if j != SSD_CONV // 2:
                acc = acc + _shifted_rows(ext, j - SSD_CONV // 2, L) * w_ref[j:j + 1, sl]
        return _silu(acc)

    for g in range(SSD_GROUPS):
        xg = conv_silu(xs_ref, xsp_ref, xsn_ref, cwx_ref, cbx_ref, g * gw)
        for q in range(gw // LANES):
            rows = slice(g * gw + q * LANES, g * gw + (q + 1) * LANES)
            x_t = xg[:, q * LANES:(q + 1) * LANES].T
            xt_ref[rows, :] = x_t
            xt_out_ref[0, 0, rows, :] = x_t.astype(BF16)
    for q in range(SSD_BC // gw):
        bc_out_ref[0, :, q * gw:(q + 1) * gw] = conv_silu(bc_ref, bcp_ref, bcn_ref, cwb_ref, cbb_ref,
                                                          q * gw).astype(BF16)

    acum_t, dt_t, tot = _ssd_decay_terms(dt_ref, dtb_ref, alog_ref)
    dec_ref[...] = jnp.exp(tot - acum_t) * dt_t
    tot_ref[...] = jnp.broadcast_to(jnp.exp(tot), (2 * SSD_HEADS, N))
    sout_ref[0, 0] = state_ref[...].astype(BF16)

    for g in range(SSD_GROUPS):
        b_g = bc_out_ref[0, :, g * N:(g + 1) * N]
        for r in range(R):
            h = g * R + r
            hrows = slice(h * P, (h + 1) * P)
            hd = h + SSD_HEADS
            xsc = (xt_ref[hrows, :] * _bcast_row(dec_ref, hd, P)).astype(BF16)
            state_ref[hrows, :] = (state_ref[hrows, :] * _bcast_row(tot_ref, hd, P)
                                   + jnp.dot(xsc, b_g, preferred_element_type=F32))


def _ssd_fwd_kernel(z_ref, xt_ref, bc_ref, dt_ref, dtb_ref, alog_ref, dskip_ref, nw_ref, sb_ref, y_ref,
                    state_ref, dec_ref, tot_ref, at_ref, ar_ref, dtt_ref, e_ref, acol_ref, ys_ref, yt_ref):
    L, P, N, R = SSD_CHUNK, SSD_HEAD_DIM, SSD_STATE, SSD_HEADS_PER_GROUP
    gw = R * P
    nt = (((1,), (1,)), ((), ()))

    @pl.when(pl.program_id(1) == 0)
    def _():
        state_ref[...] = jnp.zeros_like(state_ref)

    acum_t, dt_t, tot = _ssd_decay_terms(dt_ref, dtb_ref, alog_ref)
    dec_ref[...] = jnp.exp(tot - acum_t) * dt_t
    tot_ref[...] = jnp.broadcast_to(jnp.exp(tot), (2 * SSD_HEADS, N))
    at_ref[...] = acum_t * LOG2E
    ar_ref[...] = (acum_t - jnp.log(dt_t)) * LOG2E
    dtt_ref[...] = dt_t
    e_ref[...] = jnp.exp(acum_t)

    row = lax.broadcasted_iota(jnp.int32, (L, L), 0)
    col = lax.broadcasted_iota(jnp.int32, (L, L), 1)

    def build_acol(g):
        for r in range(R):
            hf = g * R + r
            u = jnp.where(row <= col, _bcast_row(at_ref, hf, L), _bcast_row(at_ref, hf + SSD_HEADS, L))
            acol_ref[hf] = u.T

    for g in range(SSD_GROUPS):
        build_acol(g)
    for g in range(SSD_GROUPS):
        b_g = bc_ref[0, :, g * N:(g + 1) * N]
        c_g = bc_ref[0, :, (SSD_GROUPS + g) * N:(SSD_GROUPS + g + 1) * N]
        grows = slice(g * gw, (g + 1) * gw)
        cb = lax.dot_general(c_g, b_g, nt, preferred_element_type=F32)
        ys_ref[0, grows, :] = lax.dot_general(state_ref[grows, :].astype(BF16), c_g, nt, preferred_element_type=F32)
        ys_ref[1, grows, :] = lax.dot_general(sb_ref[0, 0, grows, :], c_g, nt, preferred_element_type=F32)
        for r in range(R):
            h = g * R + r
            hf = h
            hb = h + SSD_HEADS
            hrows = slice(h * P, (h + 1) * P)
            x_t = xt_ref[0, 0, hrows, :]
            x_f = x_t.astype(F32)
            arg = acol_ref[h] - jnp.where(row >= col, _bcast_row(ar_ref, hf, L), _bcast_row(ar_ref, hb, L))
            wd = jnp.exp2(arg) + jnp.where(row == col, _bcast_row(dtt_ref, hb, L), 0.0)
            w = (cb * wd).astype(BF16)
            y_t = lax.dot_general(x_t, w, nt, preferred_element_type=F32)
            y_t = y_t + ys_ref[0, hrows, :] * _bcast_row(e_ref, hf, P)
            y_t = y_t + ys_ref[1, hrows, :] * _bcast_row(e_ref, hb, P)
            yt_ref[hrows, :] = y_t + x_f * dskip_ref[h]
            xsc = (x_f * _bcast_row(dec_ref, hf, P)).astype(BF16)
            state_ref[hrows, :] = (state_ref[hrows, :] * _bcast_row(tot_ref, hf, P)
                                   + jnp.dot(xsc, b_g, preferred_element_type=F32))
        ys = []
        ssq = None
        for q in range(gw // LANES):
            sl = slice(g * gw + q * LANES, g * gw + (q + 1) * LANES)
            yb = yt_ref[sl, :].T * _silu(z_ref[0, :, sl].astype(F32))
            ys.append(yb)
            part = jnp.sum(yb * yb, axis=-1, keepdims=True)
            ssq = part if ssq is None else ssq + part
        inv = lax.rsqrt(ssq * (1.0 / gw) + EPS)
        for q, yb in enumerate(ys):
            sl = slice(g * gw + q * LANES, g * gw + (q + 1) * LANES)
            y_ref[0, :, sl] = (yb * inv * nw_ref[:, sl]).astype(BF16)


def _ssd(rest, dt_raw, conv_w, conv_b, dt_bias, a_log, d_skip, norm_w):
    bsz, s, _ = rest.shape
    L = SSD_CHUNK
    nc = s // L
    hb = L // HALO
    n_halo = s // HALO
    cwx, cwb = conv_w[:, :SSD_INNER], conv_w[:, SSD_INNER:]
    cbx, cbb = conv_b[:, :SSD_INNER], conv_b[:, SSD_INNER:]

    rev = lambda c: nc - 1 - c
    prev = lambda col: (lambda b, c: (b, jnp.maximum(rev(c) * hb - 1, 0), col))
    nxt = lambda col: (lambda b, c: (b, jnp.minimum((rev(c) + 1) * hb, n_halo - 1), col))
    const = lambda b, c: (0, 0)
    xs_col, bc_col = REST_XS // SSD_INNER, REST_BC // SSD_BC
    state_scratch = [
        pltpu.VMEM((SSD_INNER, SSD_STATE), F32),
    ]
    decay_scratch = [
        pltpu.VMEM((2 * SSD_HEADS, L), F32),
        pltpu.VMEM((2 * SSD_HEADS, SSD_STATE), F32),
    ]

    states_b, xt, bc = pl.pallas_call(
        _ssd_bwd_kernel,
        grid=(bsz, nc),
        in_specs=[
            pl.BlockSpec((1, L, SSD_INNER), lambda b, c: (b, rev(c), xs_col)),
            pl.BlockSpec((1, HALO, SSD_INNER), prev(xs_col)),
            pl.BlockSpec((1, HALO, SSD_INNER), nxt(xs_col)),
            pl.BlockSpec((1, L, SSD_BC), lambda b, c: (b, rev(c), bc_col)),
            pl.BlockSpec((1, HALO, SSD_BC), prev(bc_col)),
            pl.BlockSpec((1, HALO, SSD_BC), nxt(bc_col)),
            pl.BlockSpec((1, L, 2 * SSD_HEADS), lambda b, c: (b, rev(c), 0)),
            pl.BlockSpec((SSD_CONV, SSD_INNER), const),
            pl.BlockSpec((1, SSD_INNER), const),
            pl.BlockSpec((SSD_CONV, SSD_BC), const),
            pl.BlockSpec((1, SSD_BC), const),
            pl.BlockSpec((1, 2 * SSD_HEADS), const),
            pl.BlockSpec((1, 2 * SSD_HEADS), const),
        ],
        out_specs=[pl.BlockSpec((1, 1, SSD_INNER, SSD_STATE), lambda b, c: (b, rev(c), 0, 0)),
                   pl.BlockSpec((1, 1, SSD_INNER, L), lambda b, c: (b, rev(c), 0, 0)),
                   pl.BlockSpec((1, L, SSD_BC), lambda b, c: (b, rev(c), 0))],
        out_shape=[jax.ShapeDtypeStruct((bsz, nc, SSD_INNER, SSD_STATE), BF16),
                   jax.ShapeDtypeStruct((bsz, nc, SSD_INNER, L), BF16),
                   jax.ShapeDtypeStruct((bsz, s, SSD_BC), BF16)],
        scratch_shapes=state_scratch + [pltpu.VMEM((SSD_INNER, L), F32)] + decay_scratch,
        compiler_params=_cparams(("parallel", "arbitrary")),
        name="ssd_bwd_states",
    )(rest, rest, rest, rest, rest, rest, dt_raw, cwx, cbx, cwb, cbb, dt_bias, a_log)

    chunk4 = lambda b, c: (b, c, 0, 0)
    y = pl.pallas_call(
        _ssd_fwd_kernel,
        grid=(bsz, nc),
        in_specs=[
            pl.BlockSpec((1, L, SSD_INNER), lambda b, c: (b, c, REST_Z // SSD_INNER)),
            pl.BlockSpec((1, 1, SSD_INNER, L), chunk4),
            pl.BlockSpec((1, L, SSD_BC), lambda b, c: (b, c, 0)),
            pl.BlockSpec((1, L, 2 * SSD_HEADS), lambda b, c: (b, c, 0)),
            pl.BlockSpec((1, 2 * SSD_HEADS), const),
            pl.BlockSpec((1, 2 * SSD_HEADS), const),
            pl.BlockSpec(memory_space=pltpu.SMEM),
            pl.BlockSpec((1, SSD_INNER), const),
            pl.BlockSpec((1, 1, SSD_INNER, SSD_STATE), chunk4),
        ],
        out_specs=pl.BlockSpec((1, L, SSD_INNER), lambda b, c: (b, c, 0)),
        out_shape=jax.ShapeDtypeStruct((bsz, s, SSD_INNER), BF16),
        scratch_shapes=state_scratch + decay_scratch + [
            pltpu.VMEM((2 * SSD_HEADS, L), F32),
            pltpu.VMEM((2 * SSD_HEADS, L), F32),
            pltpu.VMEM((2 * SSD_HEADS, L), F32),
            pltpu.VMEM((2 * SSD_HEADS, L), F32),
            pltpu.VMEM((SSD_HEADS, L, L), F32),
            pltpu.VMEM((2, SSD_INNER, L), F32),
            pltpu.VMEM((SSD_INNER, L), F32),
        ],
        compiler_params=_cparams(("parallel", "arbitrary")),
        name="ssd_fwd",
    )(rest, xt, bc, dt_raw, dt_bias, a_log, d_skip, norm_w, states_b)
    return y


MIX_TN = 512


def _mix_kernel(o0, o1, o2, l0, l1, l2, y_ref, ga_ref, gs_ref, wa_ref, ws_ref, m_ref, attn_ref):
    @pl.when(pl.program_id(1) == 0)
    def _():
        ls = [l0[...], l1[...], l2[...]]
        mx = jnp.maximum(jnp.maximum(ls[0], ls[1]), ls[2])
        es = [jnp.exp(l - mx) for l in ls]
        den = es[0] + es[1] + es[2]
        num = es[0] * o0[...] + es[1] * o1[...] + es[2] * o2[...]
        attn_ref[...] = (num / den).astype(BF16)

    a_br = jnp.dot(attn_ref[...], wa_ref[...], preferred_element_type=F32)
    s_br = jnp.dot(y_ref[...], ws_ref[...], preferred_element_type=F32)
    merged = jax.nn.sigmoid(ga_ref[...].astype(F32)) * a_br + jax.nn.sigmoid(gs_ref[...].astype(F32)) * s_br
    m_ref[...] = merged.astype(BF16)


def _mix(attn_outs, attn_lses, y2d, rest2d, w_attn_proj, w_ssd_proj):
    m = y2d.shape[0]
    tm = 512
    aspec = pl.BlockSpec((tm, ATTN_OUT_WIDTH), lambda i, j: (i, 0))
    ga0 = REST_GA // MIX_TN
    gs0 = REST_GS // MIX_TN
    return pl.pallas_call(
        _mix_kernel,
        grid=(m // tm, D_MODEL // MIX_TN),
        in_specs=[aspec] * 6 + [
            pl.BlockSpec((tm, SSD_INNER), lambda i, j: (i, 0)),
            pl.BlockSpec((tm, MIX_TN), lambda i, j: (i, ga0 + j)),
            pl.BlockSpec((tm, MIX_TN), lambda i, j: (i, gs0 + j)),
            pl.BlockSpec((ATTN_OUT_WIDTH, MIX_TN), lambda i, j: (0, j)),
            pl.BlockSpec((SSD_INNER, MIX_TN), lambda i, j: (0, j)),
        ],
        out_specs=pl.BlockSpec((tm, MIX_TN), lambda i, j: (i, j)),
        out_shape=jax.ShapeDtypeStruct((m, D_MODEL), BF16),
        scratch_shapes=[pltpu.VMEM((tm, ATTN_OUT_WIDTH), BF16)],
        compiler_params=_cparams(("parallel", "arbitrary")),
        name="mix",
    )(*attn_outs, *attn_lses, y2d, rest2d, rest2d, w_attn_proj, w_ssd_proj)


def _out_kernel(m_ref, w_ref, x_ref, mod_ref, nw_ref, x1_ref, h2_ref):
    mix = jnp.dot(m_ref[0], w_ref[...], preferred_element_type=F32)
    x1 = x_ref[0] + mod_ref[0, 2:3, :] * mix
    x1_ref[0] = x1
    h2_ref[0] = _rms_mod(x1, nw_ref[...], mod_ref[0, 4:5, :], mod_ref[0, 3:4, :]).astype(BF16)


def _out_proj(merged, w_out, x, mod, norm_w):
    bsz, s, _ = x.shape
    ts = 256
    tok = pl.BlockSpec((1, ts, D_MODEL), lambda b, i: (b, i, 0))
    return pl.pallas_call(
        _out_kernel,
        grid=(bsz, s // ts),
        in_specs=[tok,
                  pl.BlockSpec((D_MODEL, D_MODEL), lambda b, i: (0, 0)),
                  tok,
                  pl.BlockSpec((1, 6, D_MODEL), lambda b, i: (b, 0, 0)),
                  pl.BlockSpec((1, D_MODEL), lambda b, i: (0, 0))],
        out_specs=[tok, tok],
        out_shape=[jax.ShapeDtypeStruct((bsz, s, D_MODEL), F32),
                   jax.ShapeDtypeStruct((bsz, s, D_MODEL), BF16)],
        compiler_params=_cparams(("parallel", "parallel")),
        name="out_proj",
    )(merged, w_out, x, mod, norm_w.reshape(1, -1))


UP_TN = 512


def _up_kernel(hp_ref, h_ref, hn_ref, wg_ref, wv_ref, cw_ref, cb_ref, o_ref, lhs_ref, acc_ref, *, nj, tiles_per_seq):
    t = pl.program_id(0)

    @pl.when(t == 0)
    def _():
        acc_ref[1] = jnp.zeros(acc_ref.shape[1:], F32)

    @pl.when(t % nj == 0)
    def _():
        lhs_ref[0:HALO] = hp_ref[...]
        lhs_ref[HALO:HALO + TM] = h_ref[...]
        lhs_ref[HALO + TM:HALO + TM + HALO] = hn_ref[...]

    ip = jnp.maximum(t - 1, 0) // nj
    keep_prev = jnp.where(ip % tiles_per_seq > 0, 1.0, 0.0).astype(F32)
    keep_next = jnp.where(ip % tiles_per_seq < tiles_per_seq - 1, 1.0, 0.0).astype(F32)
    lo, hi = HALO - SUBLANES, HALO + TM

    def step(slot):
        ext = jnp.concatenate([acc_ref[1 - slot, 0, lo:HALO] * keep_prev, acc_ref[1 - slot, 0, HALO:hi],
                               acc_ref[1 - slot, 0, hi:hi + SUBLANES] * keep_next], axis=0)
        gate = cb_ref[0] + ext[SUBLANES:SUBLANES + TM] * cw_ref[0, FFN_CONV // 2:FFN_CONV // 2 + 1, :]
        for j in range(FFN_CONV):
            if j != FFN_CONV // 2:
                gate = gate + _shifted_rows(ext, j - FFN_CONV // 2, TM) * cw_ref[0, j:j + 1, :]
        act = 0.5 * gate * (1.0 + lax.erf(gate * (2.0 ** -0.5))) * acc_ref[1 - slot, 1, HALO:hi]
        o_ref[...] = act.astype(BF16)
        lhs = lhs_ref[...]
        acc_ref[slot, 0] = jnp.dot(lhs, wg_ref[...], preferred_element_type=F32)
        acc_ref[slot, 1] = jnp.dot(lhs, wv_ref[...], preferred_element_type=F32)

    for slot in range(2):
        pl.when(t % 2 == slot)(functools.partial(step, slot))


def _up_act(h2d, w_up, conv_w, conv_b, s):
    m = h2d.shape[0]
    ni, nj = m // TM, D_FF // UP_TN
    tiles_per_seq = s // TM
    hb = TM // HALO
    n_halo = m // HALO
    cw = conv_w.reshape(FFN_CONV, nj, UP_TN).transpose(1, 0, 2)
    cb = conv_b.reshape(nj, 1, UP_TN)
    tok = lambda t: jnp.minimum(t // nj, ni - 1)
    prev_t = lambda t: jnp.maximum(t - 1, 0)
    return pl.pallas_call(
        functools.partial(_up_kernel, nj=nj, tiles_per_seq=tiles_per_seq),
        grid=(ni * nj + 1,),
        in_specs=[pl.BlockSpec((HALO, D_MODEL), lambda t: (jnp.maximum(tok(t) * hb - 1, 0), 0)),
                  pl.BlockSpec((TM, D_MODEL), lambda t: (tok(t), 0)),
                  pl.BlockSpec((HALO, D_MODEL), lambda t: (jnp.minimum((tok(t) + 1) * hb, n_halo - 1), 0)),
                  pl.BlockSpec((D_MODEL, UP_TN), lambda t: (0, t % nj)),
                  pl.BlockSpec((D_MODEL, UP_TN), lambda t: (0, nj + t % nj)),
                  pl.BlockSpec((1, FFN_CONV, UP_TN), lambda t: (prev_t(t) % nj, 0, 0)),
                  pl.BlockSpec((1, 1, UP_TN), lambda t: (prev_t(t) % nj, 0, 0))],
        out_specs=pl.BlockSpec((TM, UP_TN), lambda t: (prev_t(t) // nj, prev_t(t) % nj)),
        out_shape=jax.ShapeDtypeStruct((m, D_FF), BF16),
        scratch_shapes=[pltpu.VMEM((TM + 2 * HALO, D_MODEL), BF16),
                        pltpu.VMEM((2, 2, TM + 2 * HALO, UP_TN), F32)],
        compiler_params=_cparams(("arbitrary",)),
        name="ffn_up",
    )(h2d, h2d, h2d, w_up, w_up, cw, cb)


DOWN_TM = 512
DOWN_TK = 1408


def _down_kernel(a_ref, w_ref, x1_ref, mod_ref, nw_ref, o_ref, acc_ref):
    k = pl.program_id(1)

    @pl.when(k == 0)
    def _():
        acc_ref[...] = jnp.zeros_like(acc_ref)

    acc_ref[...] += jnp.dot(a_ref[...], w_ref[...], preferred_element_type=F32)

    @pl.when(k == pl.num_programs(1) - 1)
    def _():
        x2 = x1_ref[...] + mod_ref[0, 5:6, :] * acc_ref[...]
        o_ref[...] = x2 * lax.rsqrt(jnp.mean(x2 * x2, axis=-1, keepdims=True) + EPS) * nw_ref[...]


def _down_proj(act2d, w_down, x1_2d, mod, norm_w, s):
    m = act2d.shape[0]
    tm, tk = DOWN_TM, DOWN_TK
    tiles_per_seq = s // tm
    return pl.pallas_call(
        _down_kernel,
        grid=(m // tm, D_FF // tk),
        in_specs=[pl.BlockSpec((tm, tk), lambda i, k: (i, k)),
                  pl.BlockSpec((tk, D_MODEL), lambda i, k: (k, 0)),
                  pl.BlockSpec((tm, D_MODEL), lambda i, k: (i, 0)),
                  pl.BlockSpec((1, 6, D_MODEL), lambda i, k: (i // tiles_per_seq, 0, 0)),
                  pl.BlockSpec((1, D_MODEL), lambda i, k: (0, 0))],
        out_specs=pl.BlockSpec((tm, D_MODEL), lambda i, k: (i, 0)),
        out_shape=jax.ShapeDtypeStruct((m, D_MODEL), F32),
        scratch_shapes=[pltpu.VMEM((tm, D_MODEL), F32)],
        compiler_params=_cparams(("parallel", "arbitrary")),
        name="ffn_down",
    )(act2d, w_down, x1_2d, mod, norm_w.reshape(1, -1))


def _rotary_tables(s):
    pos = jnp.arange(s, dtype=F32)
    inv_freq = ROPE_THETA ** (-jnp.arange(0, HEAD_DIM, 2, dtype=F32) / HEAD_DIM)
    ang = pos[:, None] * inv_freq[None, :]
    cos, sin = jnp.cos(ang), jnp.sin(ang)
    return jnp.concatenate([cos, cos], axis=-1), jnp.concatenate([-sin, sin], axis=-1)


def _prep_weights(w_in, w_attn_proj, w_ssd_proj, w_out, w_up, w_down):
    o = np.cumsum((0, ATTN_WIDTH, ATTN_WIDTH, ATTN_WIDTH, SSD_INNER, SSD_INNER + SSD_BC, 2 * SSD_HEADS,
                   D_MODEL, D_MODEL))
    wq, wk, wv = (w_in[:, o[t]:o[t + 1]].reshape(D_MODEL, N_ATTN_HEADS, HEAD_DIM) for t in range(3))
    w_qkv = jnp.stack([wq, wk, wv], axis=2).reshape(D_MODEL, QKV_WIDTH)
    w_rest = jnp.concatenate([w_in[:, o[3]:o[5]], w_in[:, o[6]:o[8]]], axis=1)
    w_dt = w_in[:, o[5]:o[6]]
    cast = lambda w: w.astype(BF16)
    return (cast(w_qkv), cast(w_rest), cast(w_dt), cast(w_attn_proj), cast(w_ssd_proj), cast(w_out),
            cast(w_up), cast(w_down))


def _trunk(x, mod, wts, small):
    bsz, s, _ = x.shape
    m = bsz * s
    w_qkv, w_rest, w_dt, w_attn_proj, w_ssd_proj, w_out, w_up, w_down = wts
    (norm_mix_w, ssd_conv_w, ssd_conv_b, dt_bias, a_log, d_skip, ssd_norm_w, norm_ffn_w, ffn_conv_w, ffn_conv_b,
     norm_f_w) = small
    h = _norm_mod(x, mod, norm_mix_w)
    h2d = h.reshape(m, D_MODEL)
    cos_t, sin_t = _rotary_tables(s)
    rest2d, dt_raw = _rest_proj(h2d, w_rest, w_dt)
    outs, lses = [], []
    for gi, (_, dilation) in enumerate(ATTN_GROUPS):
        o, l = _attention_group(_qkv_proj(h, w_qkv, cos_t, sin_t, gi, dilation), dilation)
        outs.append(o.reshape(m, ATTN_OUT_WIDTH))
        lses.append(l.reshape(m, ATTN_OUT_WIDTH))
    y = _ssd(rest2d.reshape(bsz, s, REST_WIDTH), dt_raw.reshape(bsz, s, 2 * SSD_HEADS), ssd_conv_w, ssd_conv_b,
             dt_bias, a_log, d_skip, ssd_norm_w)
    merged = _mix(outs, lses, y.reshape(m, SSD_INNER), rest2d, w_attn_proj, w_ssd_proj)
    x1, h2 = _out_proj(merged.reshape(bsz, s, D_MODEL), w_out, x, mod, norm_ffn_w)
    act = _up_act(h2.reshape(m, D_MODEL), w_up, ffn_conv_w, ffn_conv_b, s)
    out = _down_proj(act, w_down, x1.reshape(m, D_MODEL), mod, norm_f_w, s)
    return out.reshape(bsz, s, D_MODEL)


def kernel(x_prompt, x_sample, c_prompt, c_sample, w_ada, b_ada, norm_mix_w, w_in, ssd_conv_w, ssd_conv_b,
           dt_bias_fwd, dt_bias_bwd, a_log_fwd, a_log_bwd, ssd_d, ssd_norm_w, w_attn_proj, w_ssd_proj, w_out,
           norm_ffn_w, w_up, ffn_conv_w, ffn_conv_b, w_down, norm_f_w):
    wts = _prep_weights(w_in[0], w_attn_proj[0], w_ssd_proj[0], w_out[0], w_up[0], w_down[0])
    small = (norm_mix_w[0], ssd_conv_w[0], ssd_conv_b[0].reshape(1, -1),
             jnp.concatenate([dt_bias_fwd[0], dt_bias_bwd[0]]).reshape(1, -1),
             jnp.concatenate([a_log_fwd[0], a_log_bwd[0]]).reshape(1, -1),
             ssd_d[0], ssd_norm_w[0].reshape(1, -1),
             norm_ffn_w[0], ffn_conv_w[0], ffn_conv_b[0], norm_f_w)
    nb_p = c_prompt.shape[0]
    nb_s = c_sample.shape[0]
    pad = (-(nb_p + nb_s)) % 8
    c_all = jnp.concatenate([c_prompt, c_sample, jnp.zeros((pad, D_MODEL), F32)], axis=0)
    mod_all = _modulation(c_all, w_ada[0], b_ada[0]).reshape(-1, 6, D_MODEL)
    y_prompt = _trunk(x_prompt, mod_all[:nb_p], wts, small)
    y_sample = _trunk(x_sample, mod_all[nb_p:nb_p + nb_s], wts, small)
    return (y_prompt, y_sample)
```

```python
import functools

import numpy as np
import jax
import jax.numpy as jnp
from jax import lax
from jax.experimental import pallas as pl
from jax.experimental.pallas import tpu as pltpu

F32 = jnp.float32
BF16 = jnp.bfloat16

D_MODEL = 2048
HEAD_DIM = 128
ATTN_GROUPS = ((128, 1), (512, 4), (2048, 16))
HEADS_PER_GROUP = 4
N_ATTN_HEADS = HEADS_PER_GROUP * len(ATTN_GROUPS)
ATTN_WIDTH = N_ATTN_HEADS * HEAD_DIM
ATTN_OUT_WIDTH = HEADS_PER_GROUP * HEAD_DIM
ROPE_THETA = 10000.0
SSD_INNER = 2 * D_MODEL
SSD_HEAD_DIM = 64
SSD_HEADS = SSD_INNER // SSD_HEAD_DIM
SSD_GROUPS = 8
SSD_HEADS_PER_GROUP = SSD_HEADS // SSD_GROUPS
SSD_STATE = 128
SSD_CONV = 5
SSD_CHUNK = 128
SSD_BC = 2 * SSD_GROUPS * SSD_STATE
D_FF = 5632
FFN_CONV = 3
EPS = 1e-6
NEG_INF = -1e30
LOG2E = 1.4426950408889634

QKV_WIDTH = 3 * ATTN_WIDTH
REST_WIDTH = 2 * SSD_INNER + SSD_BC + 2 * D_MODEL
REST_Z, REST_XS, REST_BC, REST_GA, REST_GS = 0, SSD_INNER, 2 * SSD_INNER, 2 * SSD_INNER + SSD_BC, 2 * SSD_INNER + SSD_BC + D_MODEL

LANES = 128
SUBLANES = 8
HALO = 16
VMEM_LIMIT = 56 * 1024 * 1024
TM = 1024


def _cparams(sem):
    return pltpu.CompilerParams(dimension_semantics=sem, vmem_limit_bytes=VMEM_LIMIT)


def _silu(x):
    h = 0.5 * x
    return h + h * jnp.tanh(h)


def _shifted_rows(ext, shift, rows):
    if shift == 0:
        return ext[SUBLANES:SUBLANES + rows]
    return pltpu.roll(ext, (-shift) % ext.shape[0], 0)[SUBLANES:SUBLANES + rows]


def _with_halo(main_ref_val, prev_blk, next_blk, keep_prev, keep_next):
    prev = prev_blk.astype(F32)[HALO - SUBLANES:HALO] * keep_prev
    nxt = next_blk.astype(F32)[0:SUBLANES] * keep_next
    return jnp.concatenate([prev, main_ref_val.astype(F32), nxt], axis=0)


def _mod_kernel(c_ref, w_ref, b_ref, o_ref):
    c = c_ref[...]
    o_ref[...] = jnp.dot(c * jax.nn.sigmoid(c), w_ref[...], preferred_element_type=F32,
                         precision=lax.Precision.HIGHEST) + b_ref[...]


def _modulation(c_all, w_ada, b_ada):
    rows = c_all.shape[0]
    tn = 1024
    return pl.pallas_call(
        _mod_kernel,
        grid=(6 * D_MODEL // tn,),
        in_specs=[pl.BlockSpec((rows, D_MODEL), lambda j: (0, 0)),
                  pl.BlockSpec((D_MODEL, tn), lambda j: (0, j)),
                  pl.BlockSpec((1, tn), lambda j: (0, j))],
        out_specs=pl.BlockSpec((rows, tn), lambda j: (0, j)),
        out_shape=jax.ShapeDtypeStruct((rows, 6 * D_MODEL), F32),
        compiler_params=_cparams(("arbitrary",)),
        name="mod",
    )(c_all, w_ada, b_ada.reshape(1, -1))


def _rms_mod(x, w, scale, shift):
    y = x * lax.rsqrt(jnp.mean(x * x, axis=-1, keepdims=True) + EPS) * w
    return y * (1.0 + scale) + shift


def _norm_kernel(x_ref, mod_ref, w_ref, h_ref):
    h_ref[0] = _rms_mod(x_ref[0], w_ref[...], mod_ref[0, 1:2, :], mod_ref[0, 0:1, :]).astype(BF16)


def _norm_mod(x, mod, w):
    bsz, s, _ = x.shape
    ts = 512
    return pl.pallas_call(
        _norm_kernel,
        grid=(bsz, s // ts),
        in_specs=[pl.BlockSpec((1, ts, D_MODEL), lambda b, i: (b, i, 0)),
                  pl.BlockSpec((1, 6, D_MODEL), lambda b, i: (b, 0, 0)),
                  pl.BlockSpec((1, D_MODEL), lambda b, i: (0, 0))],
        out_specs=pl.BlockSpec((1, ts, D_MODEL), lambda b, i: (b, i, 0)),
        out_shape=jax.ShapeDtypeStruct((bsz, s, D_MODEL), BF16),
        compiler_params=_cparams(("parallel", "parallel")),
        name="norm_mix",
    )(x, mod, w.reshape(1, -1))


QKV_TN = 6 * HEAD_DIM
QKV_GROUP_WIDTH = 3 * ATTN_OUT_WIDTH


def _qkv_kernel(h_ref, w_ref, cos_ref, sin_ref, o_ref, *scratch, dilation):
    acc = jnp.dot(h_ref[0], w_ref[...], preferred_element_type=F32)
    cos = cos_ref[...]
    sin = sin_ref[...]
    tiles = QKV_TN // HEAD_DIM
    for t in range(tiles):
        sl = slice(t * HEAD_DIM, (t + 1) * HEAD_DIM)
        a = acc[:, sl]
        if t % 3 != 2:
            a = a * cos + pltpu.roll(a, HEAD_DIM // 2, 1) * sin
        if dilation == 1:
            o_ref[0, 0, :, sl] = a.astype(BF16)
        else:
            scratch[0][t] = a
    if dilation > 1:
        rows = TM // dilation
        for r in range(dilation):
            for t in range(tiles):
                sl = slice(t * HEAD_DIM, (t + 1) * HEAD_DIM)
                o_ref[0, r, :, sl] = scratch[0][t, pl.ds(r, rows, stride=dilation), :].astype(BF16)


def _qkv_proj(h, w_qkv, cos_t, sin_t, gi, dilation):
    bsz, s, _ = h.shape
    col_tiles = QKV_GROUP_WIDTH // QKV_TN
    rows = TM // dilation
    return pl.pallas_call(
        functools.partial(_qkv_kernel, dilation=dilation),
        grid=(bsz, s // TM, col_tiles),
        in_specs=[pl.BlockSpec((1, TM, D_MODEL), lambda b, i, j: (b, i, 0)),
                  pl.BlockSpec((D_MODEL, QKV_TN), lambda b, i, j: (0, gi * col_tiles + j)),
                  pl.BlockSpec((TM, HEAD_DIM), lambda b, i, j: (i, 0)),
                  pl.BlockSpec((TM, HEAD_DIM), lambda b, i, j: (i, 0))],
        out_specs=pl.BlockSpec((1, dilation, rows, QKV_TN), lambda b, i, j: (b, 0, i, j)),
        out_shape=jax.ShapeDtypeStruct((bsz, dilation, s // dilation, QKV_GROUP_WIDTH), BF16),
        scratch_shapes=[pltpu.VMEM((QKV_TN // HEAD_DIM, TM, HEAD_DIM), F32)] if dilation > 1 else [],
        compiler_params=_cparams(("parallel", "parallel", "arbitrary")),
        name=f"qkv_proj_d{dilation}",
    )(h, w_qkv, cos_t, sin_t)


REST_TN = 2048


def _rest_kernel(h_ref, w_ref, wdt_ref, o_ref, dt_ref):
    o_ref[...] = jnp.dot(h_ref[...], w_ref[...], preferred_element_type=F32).astype(BF16)

    @pl.when(pl.program_id(1) == 0)
    def _():
        dt_ref[...] = jnp.dot(h_ref[...], wdt_ref[...], preferred_element_type=F32)


def _rest_proj(h2d, w_rest, w_dt):
    m = h2d.shape[0]
    return pl.pallas_call(
        _rest_kernel,
        grid=(m // TM, REST_WIDTH // REST_TN),
        in_specs=[pl.BlockSpec((TM, D_MODEL), lambda i, j: (i, 0)),
                  pl.BlockSpec((D_MODEL, REST_TN), lambda i, j: (0, j)),
                  pl.BlockSpec((D_MODEL, 2 * SSD_HEADS), lambda i, j: (0, 0))],
        out_specs=[pl.BlockSpec((TM, REST_TN), lambda i, j: (i, j)),
                   pl.BlockSpec((TM, 2 * SSD_HEADS), lambda i, j: (i, 0))],
        out_shape=[jax.ShapeDtypeStruct((m, REST_WIDTH), BF16),
                   jax.ShapeDtypeStruct((m, 2 * SSD_HEADS), F32)],
        compiler_params=_cparams(("parallel", "arbitrary")),
        name="rest_proj",
    )(h2d, w_rest, w_dt)


ATTN_TQ = 128
ATTN_RADIUS = 64
ATTN_UNROLL = 4


def _attn_kernel(qkv_ref, o_ref, l_ref, *, n, dilation):
    tq = ATTN_TQ
    nq = n // tq
    win = min(n, tq + 2 * ATTN_RADIUS)
    scale = HEAD_DIM ** -0.5

    def block(it):
        r = it // nq
        i0 = pl.multiple_of((it % nq) * tq, tq)
        start = pl.multiple_of(jnp.clip(i0 - ATTN_RADIUS, 0, n - win), ATTN_RADIUS)
        rows = i0 + lax.broadcasted_iota(jnp.int32, (tq, win), 0)
        cols = start + lax.broadcasted_iota(jnp.int32, (tq, win), 1)
        valid = jnp.abs(cols - rows) <= ATTN_RADIUS
        q = qkv_ref[0, r, pl.ds(i0, tq), 0:HEAD_DIM]
        k = qkv_ref[0, r, pl.ds(start, win), HEAD_DIM:2 * HEAD_DIM]
        v = qkv_ref[0, r, pl.ds(start, win), 2 * HEAD_DIM:3 * HEAD_DIM]
        sc = lax.dot_general(q, k, (((1,), (1,)), ((), ())), preferred_element_type=F32) * scale
        sc = jnp.where(valid, sc, NEG_INF)
        mx = jnp.max(sc, axis=-1, keepdims=True)
        p = jnp.exp(sc - mx)
        den = jnp.sum(p, axis=-1, keepdims=True)
        o = jnp.dot(p.astype(BF16), v, preferred_element_type=F32) / den
        lse = jnp.broadcast_to(mx + jnp.log(den), (tq, HEAD_DIM))
        if dilation == 1:
            dst = pl.ds(i0, tq)
        else:
            dst = pl.ds(i0 * dilation + r, tq, stride=dilation)
        o_ref[0, dst, :] = o
        l_ref[0, dst, :] = lse

    def body(trip, carry):
        for u in range(ATTN_UNROLL):
            block(trip * ATTN_UNROLL + u)
        return carry

    lax.fori_loop(0, dilation * nq // ATTN_UNROLL, body, 0)


def _attention_group(qkv_g, dilation):
    bsz, _, n, _ = qkv_g.shape
    s = n * dilation
    assert (dilation * (n // ATTN_TQ)) % ATTN_UNROLL == 0
    ospec = pl.BlockSpec((1, s, HEAD_DIM), lambda b, hh: (b, 0, hh))
    return pl.pallas_call(
        functools.partial(_attn_kernel, n=n, dilation=dilation),
        grid=(bsz, HEADS_PER_GROUP),
        in_specs=[pl.BlockSpec((1, dilation, n, 3 * HEAD_DIM), lambda b, hh: (b, 0, 0, hh))],
        out_specs=[ospec, ospec],
        out_shape=[jax.ShapeDtypeStruct((bsz, s, ATTN_OUT_WIDTH), F32)] * 2,
        compiler_params=_cparams(("parallel", "parallel")),
        name=f"attn_d{dilation}",
    )(qkv_g)


def _ssd_decay_terms(dt_ref, dtb_ref, alog_ref):
    L = SSD_CHUNK
    dt = jax.nn.softplus(dt_ref[0] + dtb_ref[...])
    dta = dt * (-jnp.exp(alog_ref[...]))
    row = lax.broadcasted_iota(jnp.int32, (L, L), 0)
    col = lax.broadcasted_iota(jnp.int32, (L, L), 1)
    cum_f = jnp.dot((col <= row).astype(F32), dta, preferred_element_type=F32, precision=lax.Precision.HIGHEST)
    cum_b = jnp.dot((col >= row).astype(F32), dta, preferred_element_type=F32, precision=lax.Precision.HIGHEST)
    lane = lax.broadcasted_iota(jnp.int32, (L, 2 * SSD_HEADS), 1)
    acum_t = jnp.where(lane < SSD_HEADS, cum_f, cum_b).T
    dt_t = dt.T
    hrow = lax.broadcasted_iota(jnp.int32, (2 * SSD_HEADS, 1), 0)
    tot = jnp.where(hrow < SSD_HEADS, acum_t[:, L - 1:L], acum_t[:, 0:1])
    return acum_t, dt_t, tot


def _bcast_row(ref, r, rows):
    return jnp.broadcast_to(ref[pl.ds(r, 1), :], (rows, ref.shape[1]))


def _ssd_bwd_kernel(xs_ref, xsp_ref, xsn_ref, bc_ref, bcp_ref, bcn_ref, dt_ref, cwx_ref, cbx_ref, cwb_ref, cbb_ref,
                    dtb_ref, alog_ref, sout_ref, xt_out_ref, bc_out_ref, state_ref, xt_ref, dec_ref, tot_ref):
    L, P, N, R = SSD_CHUNK, SSD_HEAD_DIM, SSD_STATE, SSD_HEADS_PER_GROUP
    gw = R * P
    c = pl.program_id(1)
    nc = pl.num_programs(1)
    cc = nc - 1 - c

    @pl.when(c == 0)
    def _():
        state_ref[...] = jnp.zeros_like(state_ref)

    keep_prev = jnp.where(cc > 0, 1.0, 0.0).astype(F32)
    keep_next = jnp.where(cc < nc - 1, 1.0, 0.0).astype(F32)

    def conv_silu(main_ref, prev_ref, next_ref, w_ref, b_ref, col0):
        sl = slice(col0, col0 + gw)
        ext = _with_halo(main_ref[0, :, sl], prev_ref[0, :, sl], next_ref[0, :, sl], keep_prev, keep_next)
        acc = b_ref[:, sl] + ext[SUBLANES:SUBLANES + L] * w_ref[SSD_CONV // 2:SSD_CONV // 2 + 1, sl]
        for j in range(SSD_CONV):
            if j != SSD_CONV // 2:
                acc = acc + _shifted_rows(ext, j - SSD_CONV // 2, L) * w_ref[j:j + 1, sl]
        return _silu(acc)

    for g in range(SSD_GROUPS):
        xg = conv_silu(xs_ref, xsp_ref, xsn_ref, cwx_ref, cbx_ref, g * gw)
        for q in range(gw // LANES):
            rows = slice(g * gw + q * LANES, g * gw + (q + 1) * LANES)
            x_t = xg[:, q * LANES:(q + 1) * LANES].T
            xt_ref[rows, :] = x_t
            xt_out_ref[0, 0, rows, :] = x_t.astype(BF16)
    for q in range(SSD_BC // gw):
        bc_out_ref[0, :, q * gw:(q + 1) * gw] = conv_silu(bc_ref, bcp_ref, bcn_ref, cwb_ref, cbb_ref,
                                                          q * gw).astype(BF16)

    acum_t, dt_t, tot = _ssd_decay_terms(dt_ref, dtb_ref, alog_ref)
    dec_ref[...] = jnp.exp(tot - acum_t) * dt_t
    tot_ref[...] = jnp.broadcast_to(jnp.exp(tot), (2 * SSD_HEADS, N))
    sout_ref[0, 0] = state_ref[...].astype(BF16)

    for g in range(SSD_GROUPS):
        b_g = bc_out_ref[0, :, g * N:(g + 1) * N]
        for r in range(R):
            h = g * R + r
            hrows = slice(h * P, (h + 1) * P)
            hd = h + SSD_HEADS
            xsc = (xt_ref[hrows, :] * _bcast_row(dec_ref, hd, P)).astype(BF16)
            state_ref[hrows, :] = (state_ref[hrows, :] * _bcast_row(tot_ref, hd, P)
                                   + jnp.dot(xsc, b_g, preferred_element_type=F32))


def _ssd_fwd_kernel(z_ref, xt_ref, bc_ref, dt_ref, dtb_ref, alog_ref, dskip_ref, nw_ref, sb_ref, y_ref,
                    state_ref, dec_ref, tot_ref, at_ref, ar_ref, dtt_ref, e_ref, acol_ref, ys_ref, yt_ref):
    L, P, N, R = SSD_CHUNK, SSD_HEAD_DIM, SSD_STATE, SSD_HEADS_PER_GROUP
    gw = R * P
    nt = (((1,), (1,)), ((), ()))

    @pl.when(pl.program_id(1) == 0)
    def _():
        state_ref[...] = jnp.zeros_like(state_ref)

    acum_t, dt_t, tot = _ssd_decay_terms(dt_ref, dtb_ref, alog_ref)
    dec_ref[...] = jnp.exp(tot - acum_t) * dt_t
    tot_ref[...] = jnp.broadcast_to(jnp.exp(tot), (2 * SSD_HEADS, N))
    at_ref[...] = acum_t * LOG2E
    ar_ref[...] = (acum_t - jnp.log(dt_t)) * LOG2E
    dtt_ref[...] = dt_t
    e_ref[...] = jnp.exp(acum_t)

    row = lax.broadcasted_iota(jnp.int32, (L, L), 0)
    col = lax.broadcasted_iota(jnp.int32, (L, L), 1)

    def build_acol(g):
        for r in range(R):
            hf = g * R + r
            u = jnp.where(row <= col, _bcast_row(at_ref, hf, L), _bcast_row(at_ref, hf + SSD_HEADS, L))
            acol_ref[hf] = u.T

    for g in range(SSD_GROUPS):
        build_acol(g)
    for g in range(SSD_GROUPS):
        b_g = bc_ref[0, :, g * N:(g + 1) * N]
        c_g = bc_ref[0, :, (SSD_GROUPS + g) * N:(SSD_GROUPS + g + 1) * N]
        grows = slice(g * gw, (g + 1) * gw)
        cb = lax.dot_general(c_g, b_g, nt, preferred_element_type=F32)
        ys_ref[0, grows, :] = lax.dot_general(state_ref[grows, :].astype(BF16), c_g, nt, preferred_element_type=F32)
        ys_ref[1, grows, :] = lax.dot_general(sb_ref[0, 0, grows, :], c_g, nt, preferred_element_type=F32)
        for r in range(R):
            h = g * R + r
            hf = h
            hb = h + SSD_HEADS
            hrows = slice(h * P, (h + 1) * P)
            x_t = xt_ref[0, 0, hrows, :]
            x_f = x_t.astype(F32)
            arg = acol_ref[h] - jnp.where(row >= col, _bcast_row(ar_ref, hf, L), _bcast_row(ar_ref, hb, L))
            wd = jnp.exp2(arg) + jnp.where(row == col, _bcast_row(dtt_ref, hb, L), 0.0)
            w = (cb * wd).astype(BF16)
            y_t = lax.dot_general(x_t, w, nt, preferred_element_type=F32)
            y_t = y_t + ys_ref[0, hrows, :] * _bcast_row(e_ref, hf, P)
            y_t = y_t + ys_ref[1, hrows, :] * _bcast_row(e_ref, hb, P)
            yt_ref[hrows, :] = y_t + x_f * dskip_ref[h]
            xsc = (x_f * _bcast_row(dec_ref, hf, P)).astype(BF16)
            state_ref[hrows, :] = (state_ref[hrows, :] * _bcast_row(tot_ref, hf, P)
                                   + jnp.dot(xsc, b_g, preferred_element_type=F32))
        ys = []
        ssq = None
        for q in range(gw // LANES):
            sl = slice(g * gw + q * LANES, g * gw + (q + 1) * LANES)
            yb = yt_ref[sl, :].T * _silu(z_ref[0, :, sl].astype(F32))
            ys.append(yb)
            part = jnp.sum(yb * yb, axis=-1, keepdims=True)
            ssq = part if ssq is None else ssq + part
        inv = lax.rsqrt(ssq * (1.0 / gw) + EPS)
        for q, yb in enumerate(ys):
            sl = slice(g * gw + q * LANES, g * gw + (q + 1) * LANES)
            y_ref[0, :, sl] = (yb * inv * nw_ref[:, sl]).astype(BF16)


def _ssd(rest, dt_raw, conv_w, conv_b, dt_bias, a_log, d_skip, norm_w):
    bsz, s, _ = rest.shape
    L = SSD_CHUNK
    nc = s // L
    hb = L // HALO
    n_halo = s // HALO
    cwx, cwb = conv_w[:, :SSD_INNER], conv_w[:, SSD_INNER:]
    cbx, cbb = conv_b[:, :SSD_INNER], conv_b[:, SSD_INNER:]

    rev = lambda c: nc - 1 - c
    prev = lambda col: (lambda b, c: (b, jnp.maximum(rev(c) * hb - 1, 0), col))
    nxt = lambda col: (lambda b, c: (b, jnp.minimum((rev(c) + 1) * hb, n_halo - 1), col))
    const = lambda b, c: (0, 0)
    xs_col, bc_col = REST_XS // SSD_INNER, REST_BC // SSD_BC
    state_scratch = [
        pltpu.VMEM((SSD_INNER, SSD_STATE), F32),
    ]
    decay_scratch = [
        pltpu.VMEM((2 * SSD_HEADS, L), F32),
        pltpu.VMEM((2 * SSD_HEADS, SSD_STATE), F32),
    ]

    states_b, xt, bc = pl.pallas_call(
        _ssd_bwd_kernel,
        grid=(bsz, nc),
        in_specs=[
            pl.BlockSpec((1, L, SSD_INNER), lambda b, c: (b, rev(c), xs_col)),
            pl.BlockSpec((1, HALO, SSD_INNER), prev(xs_col)),
            pl.BlockSpec((1, HALO, SSD_INNER), nxt(xs_col)),
            pl.BlockSpec((1, L, SSD_BC), lambda b, c: (b, rev(c), bc_col)),
            pl.BlockSpec((1, HALO, SSD_BC), prev(bc_col)),
            pl.BlockSpec((1, HALO, SSD_BC), nxt(bc_col)),
            pl.BlockSpec((1, L, 2 * SSD_HEADS), lambda b, c: (b, rev(c), 0)),
            pl.BlockSpec((SSD_CONV, SSD_INNER), const),
            pl.BlockSpec((1, SSD_INNER), const),
            pl.BlockSpec((SSD_CONV, SSD_BC), const),
            pl.BlockSpec((1, SSD_BC), const),
            pl.BlockSpec((1, 2 * SSD_HEADS), const),
            pl.BlockSpec((1, 2 * SSD_HEADS), const),
        ],
        out_specs=[pl.BlockSpec((1, 1, SSD_INNER, SSD_STATE), lambda b, c: (b, rev(c), 0, 0)),
                   pl.BlockSpec((1, 1, SSD_INNER, L), lambda b, c: (b, rev(c), 0, 0)),
                   pl.BlockSpec((1, L, SSD_BC), lambda b, c: (b, rev(c), 0))],
        out_shape=[jax.ShapeDtypeStruct((bsz, nc, SSD_INNER, SSD_STATE), BF16),
                   jax.ShapeDtypeStruct((bsz, nc, SSD_INNER, L), BF16),
                   jax.ShapeDtypeStruct((bsz, s, SSD_BC), BF16)],
        scratch_shapes=state_scratch + [pltpu.VMEM((SSD_INNER, L), F32)] + decay_scratch,
        compiler_params=_cparams(("parallel", "arbitrary")),
        name="ssd_bwd_states",
    )(rest, rest, rest, rest, rest, rest, dt_raw, cwx, cbx, cwb, cbb, dt_bias, a_log)

    chunk4 = lambda b, c: (b, c, 0, 0)
    y = pl.pallas_call(
        _ssd_fwd_kernel,
        grid=(bsz, nc),
        in_specs=[
            pl.BlockSpec((1, L, SSD_INNER), lambda b, c: (b, c, REST_Z // SSD_INNER)),
            pl.BlockSpec((1, 1, SSD_INNER, L), chunk4),
            pl.BlockSpec((1, L, SSD_BC), lambda b, c: (b, c, 0)),
            pl.BlockSpec((1, L, 2 * SSD_HEADS), lambda b, c: (b, c, 0)),
            pl.BlockSpec((1, 2 * SSD_HEADS), const),
            pl.BlockSpec((1, 2 * SSD_HEADS), const),
            pl.BlockSpec(memory_space=pltpu.SMEM),
            pl.BlockSpec((1, SSD_INNER), const),
            pl.BlockSpec((1, 1, SSD_INNER, SSD_STATE), chunk4),
        ],
        out_specs=pl.BlockSpec((1, L, SSD_INNER), lambda b, c: (b, c, 0)),
        out_shape=jax.ShapeDtypeStruct((bsz, s, SSD_INNER), BF16),
        scratch_shapes=state_scratch + decay_scratch + [
            pltpu.VMEM((2 * SSD_HEADS, L), F32),
            pltpu.VMEM((2 * SSD_HEADS, L), F32),
            pltpu.VMEM((2 * SSD_HEADS, L), F32),
            pltpu.VMEM((2 * SSD_HEADS, L), F32),
            pltpu.VMEM((SSD_HEADS, L, L), F32),
            pltpu.VMEM((2, SSD_INNER, L), F32),
            pltpu.VMEM((SSD_INNER, L), F32),
        ],
        compiler_params=_cparams(("parallel", "arbitrary")),
        name="ssd_fwd",
    )(rest, xt, bc, dt_raw, dt_bias, a_log, d_skip, norm_w, states_b)
    return y


MIX_TN = 512


def _mix_kernel(o0, o1, o2, l0, l1, l2, y_ref, ga_ref, gs_ref, wa_ref, ws_ref, m_ref, attn_ref):
    @pl.when(pl.program_id(1) == 0)
    def _():
        ls = [l0[...], l1[...], l2[...]]
        mx = jnp.maximum(jnp.maximum(ls[0], ls[1]), ls[2])
        es = [jnp.exp(l - mx) for l in ls]
        den = es[0] + es[1] + es[2]
        num = es[0] * o0[...] + es[1] * o1[...] + es[2] * o2[...]
        attn_ref[...] = (num / den).astype(BF16)

    a_br = jnp.dot(attn_ref[...], wa_ref[...], preferred_element_type=F32)
    s_br = jnp.dot(y_ref[...], ws_ref[...], preferred_element_type=F32)
    merged = jax.nn.sigmoid(ga_ref[...].astype(F32)) * a_br + jax.nn.sigmoid(gs_ref[...].astype(F32)) * s_br
    m_ref[...] = merged.astype(BF16)


def _mix(attn_outs, attn_lses, y2d, rest2d, w_attn_proj, w_ssd_proj):
    m = y2d.shape[0]
    tm = 512
    aspec = pl.BlockSpec((tm, ATTN_OUT_WIDTH), lambda i, j: (i, 0))
    ga0 = REST_GA // MIX_TN
    gs0 = REST_GS // MIX_TN
    return pl.pallas_call(
        _mix_kernel,
        grid=(m // tm, D_MODEL // MIX_TN),
        in_specs=[aspec] * 6 + [
            pl.BlockSpec((tm, SSD_INNER), lambda i, j: (i, 0)),
            pl.BlockSpec((tm, MIX_TN), lambda i, j: (i, ga0 + j)),
            pl.BlockSpec((tm, MIX_TN), lambda i, j: (i, gs0 + j)),
            pl.BlockSpec((ATTN_OUT_WIDTH, MIX_TN), lambda i, j: (0, j)),
            pl.BlockSpec((SSD_INNER, MIX_TN), lambda i, j: (0, j)),
        ],
        out_specs=pl.BlockSpec((tm, MIX_TN), lambda i, j: (i, j)),
        out_shape=jax.ShapeDtypeStruct((m, D_MODEL), BF16),
        scratch_shapes=[pltpu.VMEM((tm, ATTN_OUT_WIDTH), BF16)],
        compiler_params=_cparams(("parallel", "arbitrary")),
        name="mix",
    )(*attn_outs, *attn_lses, y2d, rest2d, rest2d, w_attn_proj, w_ssd_proj)


def _out_kernel(m_ref, w_ref, x_ref, mod_ref, nw_ref, x1_ref, h2_ref):
    mix = jnp.dot(m_ref[0], w_ref[...], preferred_element_type=F32)
    x1 = x_ref[0] + mod_ref[0, 2:3, :] * mix
    x1_ref[0] = x1
    h2_ref[0] = _rms_mod(x1, nw_ref[...], mod_ref[0, 4:5, :], mod_ref[0, 3:4, :]).astype(BF16)


def _out_proj(merged, w_out, x, mod, norm_w):
    bsz, s, _ = x.shape
    ts = 256
    tok = pl.BlockSpec((1, ts, D_MODEL), lambda b, i: (b, i, 0))
    return pl.pallas_call(
        _out_kernel,
        grid=(bsz, s // ts),
        in_specs=[tok,
                  pl.BlockSpec((D_MODEL, D_MODEL), lambda b, i: (0, 0)),
                  tok,
                  pl.BlockSpec((1, 6, D_MODEL), lambda b, i: (b, 0, 0)),
                  pl.BlockSpec((1, D_MODEL), lambda b, i: (0, 0))],
        out_specs=[tok, tok],
        out_shape=[jax.ShapeDtypeStruct((bsz, s, D_MODEL), F32),
                   jax.ShapeDtypeStruct((bsz, s, D_MODEL), BF16)],
        compiler_params=_cparams(("parallel", "parallel")),
        name="out_proj",
    )(merged, w_out, x, mod, norm_w.reshape(1, -1))


UP_TN = 2816


def _up_kernel(h_ref, w_ref, o_ref):
    o_ref[...] = jnp.dot(h_ref[...], w_ref[...], preferred_element_type=F32).astype(BF16)


def _up_proj(h2d, w_up):
    m = h2d.shape[0]
    return pl.pallas_call(
        _up_kernel,
        grid=(m // TM, 2 * D_FF // UP_TN),
        in_specs=[pl.BlockSpec((TM, D_MODEL), lambda i, j: (i, 0)),
                  pl.BlockSpec((D_MODEL, UP_TN), lambda i, j: (0, j))],
        out_specs=pl.BlockSpec((TM, UP_TN), lambda i, j: (i, j)),
        out_shape=jax.ShapeDtypeStruct((m, 2 * D_FF), BF16),
        compiler_params=_cparams(("parallel", "arbitrary")),
        name="ffn_up",
    )(h2d, w_up)


DOWN_TM = 512
DOWN_TK = 1408


def _down_kernel(g_ref, gp_ref, gn_ref, v_ref, cw_ref, cb_ref, w_ref, x1_ref, mod_ref, nw_ref, o_ref,
                 acc_ref, *, tiles_per_seq):
    i = pl.program_id(0)
    k = pl.program_id(1)
    tm = DOWN_TM

    @pl.when(k == 0)
    def _():
        acc_ref[...] = jnp.zeros_like(acc_ref)

    keep_prev = jnp.where(i % tiles_per_seq > 0, 1.0, 0.0).astype(F32)
    keep_next = jnp.where(i % tiles_per_seq < tiles_per_seq - 1, 1.0, 0.0).astype(F32)
    ext = _with_halo(g_ref[...], gp_ref[...], gn_ref[...], keep_prev, keep_next)
    gate = cb_ref[0] + ext[SUBLANES:SUBLANES + tm] * cw_ref[0, FFN_CONV // 2:FFN_CONV // 2 + 1, :]
    for j in range(FFN_CONV):
        if j != FFN_CONV // 2:
            gate = gate + _shifted_rows(ext, j - FFN_CONV // 2, tm) * cw_ref[0, j:j + 1, :]
    act = 0.5 * gate * (1.0 + lax.erf(gate * (2.0 ** -0.5))) * v_ref[...].astype(F32)
    acc_ref[...] += jnp.dot(act.astype(BF16), w_ref[...], preferred_element_type=F32)

    @pl.when(k == pl.num_programs(1) - 1)
    def _():
        x2 = x1_ref[...] + mod_ref[0, 5:6, :] * acc_ref[...]
        o_ref[...] = x2 * lax.rsqrt(jnp.mean(x2 * x2, axis=-1, keepdims=True) + EPS) * nw_ref[...]


def _down_proj(up2d, conv_w, conv_b, w_down, x1_2d, mod, norm_w, s):
    m = up2d.shape[0]
    tm, tk = DOWN_TM, DOWN_TK
    nk = D_FF // tk
    tiles_per_seq = s // tm
    hb = tm // HALO
    n_halo = m // HALO
    cw = conv_w.reshape(FFN_CONV, nk, tk).transpose(1, 0, 2)
    cb = conv_b.reshape(nk, 1, tk)
    return pl.pallas_call(
        functools.partial(_down_kernel, tiles_per_seq=tiles_per_seq),
        grid=(m // tm, nk),
        in_specs=[pl.BlockSpec((tm, tk), lambda i, k: (i, k)),
                  pl.BlockSpec((HALO, tk), lambda i, k: (jnp.maximum(i * hb - 1, 0), k)),
                  pl.BlockSpec((HALO, tk), lambda i, k: (jnp.minimum((i + 1) * hb, n_halo - 1), k)),
                  pl.BlockSpec((tm, tk), lambda i, k: (i, nk + k)),
                  pl.BlockSpec((1, FFN_CONV, tk), lambda i, k: (k, 0, 0)),
                  pl.BlockSpec((1, 1, tk), lambda i, k: (k, 0, 0)),
                  pl.BlockSpec((tk, D_MODEL), lambda i, k: (k, 0)),
                  pl.BlockSpec((tm, D_MODEL), lambda i, k: (i, 0)),
                  pl.BlockSpec((1, 6, D_MODEL), lambda i, k: (i // tiles_per_seq, 0, 0)),
                  pl.BlockSpec((1, D_MODEL), lambda i, k: (0, 0))],
        out_specs=pl.BlockSpec((tm, D_MODEL), lambda i, k: (i, 0)),
        out_shape=jax.ShapeDtypeStruct((m, D_MODEL), F32),
        scratch_shapes=[pltpu.VMEM((tm, D_MODEL), F32)],
        compiler_params=_cparams(("parallel", "arbitrary")),
        name="ffn_down",
    )(up2d, up2d, up2d, up2d, cw, cb, w_down, x1_2d, mod, norm_w.reshape(1, -1))


def _rotary_tables(s):
    pos = jnp.arange(s, dtype=F32)
    inv_freq = ROPE_THETA ** (-jnp.arange(0, HEAD_DIM, 2, dtype=F32) / HEAD_DIM)
    ang = pos[:, None] * inv_freq[None, :]
    cos, sin = jnp.cos(ang), jnp.sin(ang)
    return jnp.concatenate([cos, cos], axis=-1), jnp.concatenate([-sin, sin], axis=-1)


def _prep_weights(w_in, w_attn_proj, w_ssd_proj, w_out, w_up, w_down):
    o = np.cumsum((0, ATTN_WIDTH, ATTN_WIDTH, ATTN_WIDTH, SSD_INNER, SSD_INNER + SSD_BC, 2 * SSD_HEADS,
                   D_MODEL, D_MODEL))
    wq, wk, wv = (w_in[:, o[t]:o[t + 1]].reshape(D_MODEL, N_ATTN_HEADS, HEAD_DIM) for t in range(3))
    w_qkv = jnp.stack([wq, wk, wv], axis=2).reshape(D_MODEL, QKV_WIDTH)
    w_rest = jnp.concatenate([w_in[:, o[3]:o[5]], w_in[:, o[6]:o[8]]], axis=1)
    w_dt = w_in[:, o[5]:o[6]]
    cast = lambda w: w.astype(BF16)
    return (cast(w_qkv), cast(w_rest), cast(w_dt), cast(w_attn_proj), cast(w_ssd_proj), cast(w_out),
            cast(w_up), cast(w_down))


def _trunk(x, mod, wts, small):
    bsz, s, _ = x.shape
    m = bsz * s
    w_qkv, w_rest, w_dt, w_attn_proj, w_ssd_proj, w_out, w_up, w_down = wts
    (norm_mix_w, ssd_conv_w, ssd_conv_b, dt_bias, a_log, d_skip, ssd_norm_w, norm_ffn_w, ffn_conv_w, ffn_conv_b,
     norm_f_w) = small
    h = _norm_mod(x, mod, norm_mix_w)
    h2d = h.reshape(m, D_MODEL)
    cos_t, sin_t = _rotary_tables(s)
    rest2d, dt_raw = _rest_proj(h2d, w_rest, w_dt)
    outs, lses = [], []
    for gi, (_, dilation) in enumerate(ATTN_GROUPS):
        o, l = _attention_group(_qkv_proj(h, w_qkv, cos_t, sin_t, gi, dilation), dilation)
        outs.append(o.reshape(m, ATTN_OUT_WIDTH))
        lses.append(l.reshape(m, ATTN_OUT_WIDTH))
    y = _ssd(rest2d.reshape(bsz, s, REST_WIDTH), dt_raw.reshape(bsz, s, 2 * SSD_HEADS), ssd_conv_w, ssd_conv_b,
             dt_bias, a_log, d_skip, ssd_norm_w)
    merged = _mix(outs, lses, y.reshape(m, SSD_INNER), rest2d, w_attn_proj, w_ssd_proj)
    x1, h2 = _out_proj(merged.reshape(bsz, s, D_MODEL), w_out, x, mod, norm_ffn_w)
    up = _up_proj(h2.reshape(m, D_MODEL), w_up)
    out = _down_proj(up, ffn_conv_w, ffn_conv_b, w_down, x1.reshape(m, D_MODEL), mod, norm_f_w, s)
    return out.reshape(bsz, s, D_MODEL)


def kernel(x_prompt, x_sample, c_prompt, c_sample, w_ada, b_ada, norm_mix_w, w_in, ssd_conv_w, ssd_conv_b,
           dt_bias_fwd, dt_bias_bwd, a_log_fwd, a_log_bwd, ssd_d, ssd_norm_w, w_attn_proj, w_ssd_proj, w_out,
           norm_ffn_w, w_up, ffn_conv_w, ffn_conv_b, w_down, norm_f_w):
    wts = _prep_weights(w_in[0], w_attn_proj[0], w_ssd_proj[0], w_out[0], w_up[0], w_down[0])
    small = (norm_mix_w[0], ssd_conv_w[0], ssd_conv_b[0].reshape(1, -1),
             jnp.concatenate([dt_bias_fwd[0], dt_bias_bwd[0]]).reshape(1, -1),
             jnp.concatenate([a_log_fwd[0], a_log_bwd[0]]).reshape(1, -1),
             ssd_d[0], ssd_norm_w[0].reshape(1, -1),
             norm_ffn_w[0], ffn_conv_w[0], ffn_conv_b[0], norm_f_w)
    nb_p = c_prompt.shape[0]
    nb_s = c_sample.shape[0]
    pad = (-(nb_p + nb_s)) % 8
    c_all = jnp.concatenate([c_prompt, c_sample, jnp.zeros((pad, D_MODEL), F32)], axis=0)
    mod_all = _modulation(c_all, w_ada[0], b_ada[0]).reshape(-1, 6, D_MODEL)
    y_prompt = _trunk(x_prompt, mod_all[:nb_p], wts, small)
    y_sample = _trunk(x_sample, mod_all[nb_p:nb_p + nb_s], wts, small)
    return (y_prompt, y_sample)
```

```python
import functools

import numpy as np
import jax
import jax.numpy as jnp
from jax import lax
from jax.experimental import pallas as pl
from jax.experimental.pallas import tpu as pltpu

F32 = jnp.float32
BF16 = jnp.bfloat16

D_MODEL = 2048
HEAD_DIM = 128
ATTN_GROUPS = ((128, 1), (512, 4), (2048, 16))
HEADS_PER_GROUP = 4
N_ATTN_HEADS = HEADS_PER_GROUP * len(ATTN_GROUPS)
ATTN_WIDTH = N_ATTN_HEADS * HEAD_DIM
ATTN_OUT_WIDTH = HEADS_PER_GROUP * HEAD_DIM
ROPE_THETA = 10000.0
SSD_INNER = 2 * D_MODEL
SSD_HEAD_DIM = 64
SSD_HEADS = SSD_INNER // SSD_HEAD_DIM
SSD_GROUPS = 8
SSD_HEADS_PER_GROUP = SSD_HEADS // SSD_GROUPS
SSD_STATE = 128
SSD_CONV = 5
SSD_CHUNK = 128
SSD_BC = 2 * SSD_GROUPS * SSD_STATE
D_FF = 5632
FFN_CONV = 3
EPS = 1e-6
NEG_INF = -1e30
LOG2E = 1.4426950408889634

QKV_WIDTH = 3 * ATTN_WIDTH
REST_WIDTH = 2 * SSD_INNER + SSD_BC + 2 * D_MODEL
REST_Z, REST_XS, REST_BC, REST_GA, REST_GS = 0, SSD_INNER, 2 * SSD_INNER, 2 * SSD_INNER + SSD_BC, 2 * SSD_INNER + SSD_BC + D_MODEL

LANES = 128
SUBLANES = 8
HALO = 16
VMEM_LIMIT = 56 * 1024 * 1024
TM = 1024


def _cparams(sem):
    return pltpu.CompilerParams(dimension_semantics=sem, vmem_limit_bytes=VMEM_LIMIT)


def _silu(x):
    h = 0.5 * x
    return h + h * jnp.tanh(h)


def _shifted_rows(ext, shift, rows):
    if shift == 0:
        return ext[SUBLANES:SUBLANES + rows]
    return pltpu.roll(ext, (-shift) % ext.shape[0], 0)[SUBLANES:SUBLANES + rows]


def _with_halo(main_ref_val, prev_blk, next_blk, keep_prev, keep_next):
    prev = prev_blk.astype(F32)[HALO - SUBLANES:HALO] * keep_prev
    nxt = next_blk.astype(F32)[0:SUBLANES] * keep_next
    return jnp.concatenate([prev, main_ref_val.astype(F32), nxt], axis=0)


def _mod_kernel(c_ref, w_ref, b_ref, o_ref):
    c = c_ref[...]
    o_ref[...] = jnp.dot(c * jax.nn.sigmoid(c), w_ref[...], preferred_element_type=F32,
                         precision=lax.Precision.HIGHEST) + b_ref[...]


def _modulation(c_all, w_ada, b_ada):
    rows = c_all.shape[0]
    tn = 1024
    return pl.pallas_call(
        _mod_kernel,
        grid=(6 * D_MODEL // tn,),
        in_specs=[pl.BlockSpec((rows, D_MODEL), lambda j: (0, 0)),
                  pl.BlockSpec((D_MODEL, tn), lambda j: (0, j)),
                  pl.BlockSpec((1, tn), lambda j: (0, j))],
        out_specs=pl.BlockSpec((rows, tn), lambda j: (0, j)),
        out_shape=jax.ShapeDtypeStruct((rows, 6 * D_MODEL), F32),
        compiler_params=_cparams(("arbitrary",)),
        name="mod",
    )(c_all, w_ada, b_ada.reshape(1, -1))


def _rms_mod(x, w, scale, shift):
    y = x * lax.rsqrt(jnp.mean(x * x, axis=-1, keepdims=True) + EPS) * w
    return y * (1.0 + scale) + shift


def _norm_kernel(x_ref, mod_ref, w_ref, h_ref):
    h_ref[0] = _rms_mod(x_ref[0], w_ref[...], mod_ref[0, 1:2, :], mod_ref[0, 0:1, :]).astype(BF16)


def _norm_mod(x, mod, w):
    bsz, s, _ = x.shape
    ts = 512
    return pl.pallas_call(
        _norm_kernel,
        grid=(bsz, s // ts),
        in_specs=[pl.BlockSpec((1, ts, D_MODEL), lambda b, i: (b, i, 0)),
                  pl.BlockSpec((1, 6, D_MODEL), lambda b, i: (b, 0, 0)),
                  pl.BlockSpec((1, D_MODEL), lambda b, i: (0, 0))],
        out_specs=pl.BlockSpec((1, ts, D_MODEL), lambda b, i: (b, i, 0)),
        out_shape=jax.ShapeDtypeStruct((bsz, s, D_MODEL), BF16),
        compiler_params=_cparams(("parallel", "parallel")),
        name="norm_mix",
    )(x, mod, w.reshape(1, -1))


QKV_GROUP_WIDTH = 3 * ATTN_OUT_WIDTH
QKV_TN = QKV_GROUP_WIDTH


def _qkv_kernel(h_ref, w_ref, cos_ref, sin_ref, o_ref, *scratch, dilation):
    acc = jnp.dot(h_ref[0], w_ref[...], preferred_element_type=F32)
    cos = cos_ref[...]
    sin = sin_ref[...]
    tiles = QKV_TN // HEAD_DIM
    for t in range(tiles):
        sl = slice(t * HEAD_DIM, (t + 1) * HEAD_DIM)
        a = acc[:, sl]
        if t % 3 != 2:
            a = a * cos + pltpu.roll(a, HEAD_DIM // 2, 1) * sin
        if dilation == 1:
            o_ref[0, 0, :, sl] = a.astype(BF16)
        else:
            scratch[0][t] = a
    if dilation > 1:
        rows = TM // dilation
        for r in range(dilation):
            for t in range(tiles):
                sl = slice(t * HEAD_DIM, (t + 1) * HEAD_DIM)
                o_ref[0, r, :, sl] = scratch[0][t, pl.ds(r, rows, stride=dilation), :].astype(BF16)


def _qkv_proj(h, w_qkv, cos_t, sin_t, gi, dilation):
    bsz, s, _ = h.shape
    col_tiles = QKV_GROUP_WIDTH // QKV_TN
    rows = TM // dilation
    return pl.pallas_call(
        functools.partial(_qkv_kernel, dilation=dilation),
        grid=(bsz, s // TM, col_tiles),
        in_specs=[pl.BlockSpec((1, TM, D_MODEL), lambda b, i, j: (b, i, 0)),
                  pl.BlockSpec((D_MODEL, QKV_TN), lambda b, i, j: (0, gi * col_tiles + j)),
                  pl.BlockSpec((TM, HEAD_DIM), lambda b, i, j: (i, 0)),
                  pl.BlockSpec((TM, HEAD_DIM), lambda b, i, j: (i, 0))],
        out_specs=pl.BlockSpec((1, dilation, rows, QKV_TN), lambda b, i, j: (b, 0, i, j)),
        out_shape=jax.ShapeDtypeStruct((bsz, dilation, s // dilation, QKV_GROUP_WIDTH), BF16),
        scratch_shapes=[pltpu.VMEM((QKV_TN // HEAD_DIM, TM, HEAD_DIM), F32)] if dilation > 1 else [],
        compiler_params=_cparams(("parallel", "parallel", "arbitrary")),
        name=f"qkv_proj_d{dilation}",
    )(h, w_qkv, cos_t, sin_t)


REST_TN = 2048


def _rest_kernel(h_ref, w_ref, wdt_ref, o_ref, dt_ref):
    o_ref[...] = jnp.dot(h_ref[...], w_ref[...], preferred_element_type=F32).astype(BF16)

    @pl.when(pl.program_id(1) == 0)
    def _():
        dt_ref[...] = jnp.dot(h_ref[...], wdt_ref[...], preferred_element_type=F32)


def _rest_proj(h2d, w_rest, w_dt):
    m = h2d.shape[0]
    return pl.pallas_call(
        _rest_kernel,
        grid=(m // TM, REST_WIDTH // REST_TN),
        in_specs=[pl.BlockSpec((TM, D_MODEL), lambda i, j: (i, 0)),
                  pl.BlockSpec((D_MODEL, REST_TN), lambda i, j: (0, j)),
                  pl.BlockSpec((D_MODEL, 2 * SSD_HEADS), lambda i, j: (0, 0))],
        out_specs=[pl.BlockSpec((TM, REST_TN), lambda i, j: (i, j)),
                   pl.BlockSpec((TM, 2 * SSD_HEADS), lambda i, j: (i, 0))],
        out_shape=[jax.ShapeDtypeStruct((m, REST_WIDTH), BF16),
                   jax.ShapeDtypeStruct((m, 2 * SSD_HEADS), F32)],
        compiler_params=_cparams(("parallel", "arbitrary")),
        name="rest_proj",
    )(h2d, w_rest, w_dt)


ATTN_TQ = 128
ATTN_RADIUS = 64
ATTN_UNROLL = 4


def _attn_kernel(qkv_ref, o_ref, l_ref, *, n, dilation):
    tq = ATTN_TQ
    nq = n // tq
    win = min(n, tq + 2 * ATTN_RADIUS)
    scale = HEAD_DIM ** -0.5

    def block(it):
        r = it // nq
        i0 = pl.multiple_of((it % nq) * tq, tq)
        start = pl.multiple_of(jnp.clip(i0 - ATTN_RADIUS, 0, n - win), ATTN_RADIUS)
        rows = i0 + lax.broadcasted_iota(jnp.int32, (tq, win), 0)
        cols = start + lax.broadcasted_iota(jnp.int32, (tq, win), 1)
        valid = jnp.abs(cols - rows) <= ATTN_RADIUS
        q = qkv_ref[0, r, pl.ds(i0, tq), 0:HEAD_DIM]
        k = qkv_ref[0, r, pl.ds(start, win), HEAD_DIM:2 * HEAD_DIM]
        v = qkv_ref[0, r, pl.ds(start, win), 2 * HEAD_DIM:3 * HEAD_DIM]
        sc = lax.dot_general(q, k, (((1,), (1,)), ((), ())), preferred_element_type=F32) * scale
        sc = jnp.where(valid, sc, NEG_INF)
        mx = jnp.max(sc, axis=-1, keepdims=True)
        p = jnp.exp(sc - mx)
        den = jnp.sum(p, axis=-1, keepdims=True)
        o = jnp.dot(p.astype(BF16), v, preferred_element_type=F32) / den
        lse = jnp.broadcast_to(mx + jnp.log(den), (tq, HEAD_DIM))
        if dilation == 1:
            dst = pl.ds(i0, tq)
        else:
            dst = pl.ds(i0 * dilation + r, tq, stride=dilation)
        o_ref[0, dst, :] = o
        l_ref[0, dst, :] = lse

    def body(trip, carry):
        for u in range(ATTN_UNROLL):
            block(trip * ATTN_UNROLL + u)
        return carry

    lax.fori_loop(0, dilation * nq // ATTN_UNROLL, body, 0)


def _attention_group(qkv_g, dilation):
    bsz, _, n, _ = qkv_g.shape
    s = n * dilation
    assert (dilation * (n // ATTN_TQ)) % ATTN_UNROLL == 0
    ospec = pl.BlockSpec((1, s, HEAD_DIM), lambda b, hh: (b, 0, hh))
    return pl.pallas_call(
        functools.partial(_attn_kernel, n=n, dilation=dilation),
        grid=(bsz, HEADS_PER_GROUP),
        in_specs=[pl.BlockSpec((1, dilation, n, 3 * HEAD_DIM), lambda b, hh: (b, 0, 0, hh))],
        out_specs=[ospec, ospec],
        out_shape=[jax.ShapeDtypeStruct((bsz, s, ATTN_OUT_WIDTH), F32)] * 2,
        compiler_params=_cparams(("parallel", "parallel")),
        name=f"attn_d{dilation}",
    )(qkv_g)


def _ssd_decay_terms(dt_ref, dtb_ref, alog_ref):
    L = SSD_CHUNK
    dt = jax.nn.softplus(dt_ref[0] + dtb_ref[...])
    dta = dt * (-jnp.exp(alog_ref[...]))
    row = lax.broadcasted_iota(jnp.int32, (L, L), 0)
    col = lax.broadcasted_iota(jnp.int32, (L, L), 1)
    cum_f = jnp.dot((col <= row).astype(F32), dta, preferred_element_type=F32, precision=lax.Precision.HIGHEST)
    cum_b = jnp.dot((col >= row).astype(F32), dta, preferred_element_type=F32, precision=lax.Precision.HIGHEST)
    lane = lax.broadcasted_iota(jnp.int32, (L, 2 * SSD_HEADS), 1)
    acum_t = jnp.where(lane < SSD_HEADS, cum_f, cum_b).T
    dt_t = dt.T
    hrow = lax.broadcasted_iota(jnp.int32, (2 * SSD_HEADS, 1), 0)
    tot = jnp.where(hrow < SSD_HEADS, acum_t[:, L - 1:L], acum_t[:, 0:1])
    return acum_t, dt_t, tot


def _bcast_row(ref, r, rows):
    return jnp.broadcast_to(ref[pl.ds(r, 1), :], (rows, ref.shape[1]))


def _ssd_bwd_kernel(xs_ref, xsp_ref, xsn_ref, bc_ref, bcp_ref, bcn_ref, dt_ref, cwx_ref, cbx_ref, cwb_ref, cbb_ref,
                    dtb_ref, alog_ref, sout_ref, xt_out_ref, bc_out_ref, state_ref, xt_ref, dec_ref, tot_ref):
    L, P, N, R = SSD_CHUNK, SSD_HEAD_DIM, SSD_STATE, SSD_HEADS_PER_GROUP
    gw = R * P
    c = pl.program_id(1)
    nc = pl.num_programs(1)
    cc = nc - 1 - c

    @pl.when(c == 0)
    def _():
        state_ref[...] = jnp.zeros_like(state_ref)

    keep_prev = jnp.where(cc > 0, 1.0, 0.0).astype(F32)
    keep_next = jnp.where(cc < nc - 1, 1.0, 0.0).astype(F32)

    def conv_silu(main_ref, prev_ref, next_ref, w_ref, b_ref, col0):
        sl = slice(col0, col0 + gw)
        ext = _with_halo(main_ref[0, :, sl], prev_ref[0, :, sl], next_ref[0, :, sl], keep_prev, keep_next)
        acc = b_ref[:, sl] + ext[SUBLANES:SUBLANES + L] * w_ref[SSD_CONV // 2:SSD_CONV // 2 + 1, sl]
        for j in range(SSD_CONV):
            if j != SSD_CONV // 2:
                acc = acc + _shifted_rows(ext, j - SSD_CONV // 2, L) * w_ref[j:j + 1, sl]
        return _silu(acc)

    for g in range(SSD_GROUPS):
        xg = conv_silu(xs_ref, xsp_ref, xsn_ref, cwx_ref, cbx_ref, g * gw)
        for q in range(gw // LANES):
            rows = slice(g * gw + q * LANES, g * gw + (q + 1) * LANES)
            x_t = xg[:, q * LANES:(q + 1) * LANES].T
            xt_ref[rows, :] = x_t
            xt_out_ref[0, 0, rows, :] = x_t.astype(BF16)
    for q in range(SSD_BC // gw):
        bc_out_ref[0, :, q * gw:(q + 1) * gw] = conv_silu(bc_ref, bcp_ref, bcn_ref, cwb_ref, cbb_ref,
                                                          q * gw).astype(BF16)

    acum_t, dt_t, tot = _ssd_decay_terms(dt_ref, dtb_ref, alog_ref)
    dec_ref[...] = jnp.exp(tot - acum_t) * dt_t
    tot_ref[...] = jnp.broadcast_to(jnp.exp(tot), (2 * SSD_HEADS, N))
    sout_ref[0, 0] = state_ref[...].astype(BF16)

    for g in range(SSD_GROUPS):
        b_g = bc_out_ref[0, :, g * N:(g + 1) * N]
        for r in range(R):
            h = g * R + r
            hrows = slice(h * P, (h + 1) * P)
            hd = h + SSD_HEADS
            xsc = (xt_ref[hrows, :] * _bcast_row(dec_ref, hd, P)).astype(BF16)
            state_ref[hrows, :] = (state_ref[hrows, :] * _bcast_row(tot_ref, hd, P)
                                   + jnp.dot(xsc, b_g, preferred_element_type=F32))


def _ssd_fwd_kernel(z_ref, xt_ref, bc_ref, dt_ref, dtb_ref, alog_ref, dskip_ref, nw_ref, sb_ref, y_ref,
                    state_ref, dec_ref, tot_ref, at_ref, ar_ref, dtt_ref, e_ref, acol_ref, ys_ref, yt_ref):
    L, P, N, R = SSD_CHUNK, SSD_HEAD_DIM, SSD_STATE, SSD_HEADS_PER_GROUP
    gw = R * P
    nt = (((1,), (1,)), ((), ()))

    @pl.when(pl.program_id(1) == 0)
    def _():
        state_ref[...] = jnp.zeros_like(state_ref)

    acum_t, dt_t, tot = _ssd_decay_terms(dt_ref, dtb_ref, alog_ref)
    dec_ref[...] = jnp.exp(tot - acum_t) * dt_t
    tot_ref[...] = jnp.broadcast_to(jnp.exp(tot), (2 * SSD_HEADS, N))
    at_ref[...] = acum_t * LOG2E
    ar_ref[...] = (acum_t - jnp.log(dt_t)) * LOG2E
    dtt_ref[...] = dt_t
    e_ref[...] = jnp.exp(acum_t)

    row = lax.broadcasted_iota(jnp.int32, (L, L), 0)
    col = lax.broadcasted_iota(jnp.int32, (L, L), 1)

    def build_acol(g):
        for r in range(R):
            hf = g * R + r
            u = jnp.where(row <= col, _bcast_row(at_ref, hf, L), _bcast_row(at_ref, hf + SSD_HEADS, L))
            acol_ref[hf] = u.T

    for g in range(SSD_GROUPS):
        build_acol(g)
    for g in range(SSD_GROUPS):
        b_g = bc_ref[0, :, g * N:(g + 1) * N]
        c_g = bc_ref[0, :, (SSD_GROUPS + g) * N:(SSD_GROUPS + g + 1) * N]
        grows = slice(g * gw, (g + 1) * gw)
        cb = lax.dot_general(c_g, b_g, nt, preferred_element_type=F32)
        cb_diag = jnp.sum(jnp.where(row == col, cb, 0.0), axis=0, keepdims=True)
        ys_ref[0, grows, :] = lax.dot_general(state_ref[grows, :].astype(BF16), c_g, nt, preferred_element_type=F32)
        ys_ref[1, grows, :] = lax.dot_general(sb_ref[0, 0, grows, :], c_g, nt, preferred_element_type=F32)
        for r in range(R):
            h = g * R + r
            hf = h
            hb = h + SSD_HEADS
            hrows = slice(h * P, (h + 1) * P)
            x_t = xt_ref[0, 0, hrows, :]
            x_f = x_t.astype(F32)
            arg = acol_ref[h] - jnp.where(row >= col, _bcast_row(ar_ref, hf, L), _bcast_row(ar_ref, hb, L))
            w = (cb * jnp.exp2(arg)).astype(BF16)
            y_t = lax.dot_general(x_t, w, nt, preferred_element_type=F32)
            y_t = y_t + ys_ref[0, hrows, :] * _bcast_row(e_ref, hf, P)
            y_t = y_t + ys_ref[1, hrows, :] * _bcast_row(e_ref, hb, P)
            yt_ref[hrows, :] = y_t + x_f * (dskip_ref[h] + cb_diag * dtt_ref[hb:hb + 1, :])
            xsc = (x_f * _bcast_row(dec_ref, hf, P)).astype(BF16)
            state_ref[hrows, :] = (state_ref[hrows, :] * _bcast_row(tot_ref, hf, P)
                                   + jnp.dot(xsc, b_g, preferred_element_type=F32))
        ys = []
        ssq = None
        for q in range(gw // LANES):
            sl = slice(g * gw + q * LANES, g * gw + (q + 1) * LANES)
            yb = yt_ref[sl, :].T * _silu(z_ref[0, :, sl].astype(F32))
            ys.append(yb)
            part = jnp.sum(yb * yb, axis=-1, keepdims=True)
            ssq = part if ssq is None else ssq + part
        inv = lax.rsqrt(ssq * (1.0 / gw) + EPS)
        for q, yb in enumerate(ys):
            sl = slice(g * gw + q * LANES, g * gw + (q + 1) * LANES)
            y_ref[0, :, sl] = (yb * inv * nw_ref[:, sl]).astype(BF16)


def _ssd(rest, dt_raw, conv_w, conv_b, dt_bias, a_log, d_skip, norm_w):
    bsz, s, _ = rest.shape
    L = SSD_CHUNK
    nc = s // L
    hb = L // HALO
    n_halo = s // HALO
    cwx, cwb = conv_w[:, :SSD_INNER], conv_w[:, SSD_INNER:]
    cbx, cbb = conv_b[:, :SSD_INNER], conv_b[:, SSD_INNER:]

    rev = lambda c: nc - 1 - c
    prev = lambda col: (lambda b, c: (b, jnp.maximum(rev(c) * hb - 1, 0), col))
    nxt = lambda col: (lambda b, c: (b, jnp.minimum((rev(c) + 1) * hb, n_halo - 1), col))
    const = lambda b, c: (0, 0)
    xs_col, bc_col = REST_XS // SSD_INNER, REST_BC // SSD_BC
    state_scratch = [
        pltpu.VMEM((SSD_INNER, SSD_STATE), F32),
    ]
    decay_scratch = [
        pltpu.VMEM((2 * SSD_HEADS, L), F32),
        pltpu.VMEM((2 * SSD_HEADS, SSD_STATE), F32),
    ]

    states_b, xt, bc = pl.pallas_call(
        _ssd_bwd_kernel,
        grid=(bsz, nc),
        in_specs=[
            pl.BlockSpec((1, L, SSD_INNER), lambda b, c: (b, rev(c), xs_col)),
            pl.BlockSpec((1, HALO, SSD_INNER), prev(xs_col)),
            pl.BlockSpec((1, HALO, SSD_INNER), nxt(xs_col)),
            pl.BlockSpec((1, L, SSD_BC), lambda b, c: (b, rev(c), bc_col)),
            pl.BlockSpec((1, HALO, SSD_BC), prev(bc_col)),
            pl.BlockSpec((1, HALO, SSD_BC), nxt(bc_col)),
            pl.BlockSpec((1, L, 2 * SSD_HEADS), lambda b, c: (b, rev(c), 0)),
            pl.BlockSpec((SSD_CONV, SSD_INNER), const),
            pl.BlockSpec((1, SSD_INNER), const),
            pl.BlockSpec((SSD_CONV, SSD_BC), const),
            pl.BlockSpec((1, SSD_BC), const),
            pl.BlockSpec((1, 2 * SSD_HEADS), const),
            pl.BlockSpec((1, 2 * SSD_HEADS), const),
        ],
        out_specs=[pl.BlockSpec((1, 1, SSD_INNER, SSD_STATE), lambda b, c: (b, rev(c), 0, 0)),
                   pl.BlockSpec((1, 1, SSD_INNER, L), lambda b, c: (b, rev(c), 0, 0)),
                   pl.BlockSpec((1, L, SSD_BC), lambda b, c: (b, rev(c), 0))],
        out_shape=[jax.ShapeDtypeStruct((bsz, nc, SSD_INNER, SSD_STATE), BF16),
                   jax.ShapeDtypeStruct((bsz, nc, SSD_INNER, L), BF16),
                   jax.ShapeDtypeStruct((bsz, s, SSD_BC), BF16)],
        scratch_shapes=state_scratch + [pltpu.VMEM((SSD_INNER, L), F32)] + decay_scratch,
        compiler_params=_cparams(("parallel", "arbitrary")),
        name="ssd_bwd_states",
    )(rest, rest, rest, rest, rest, rest, dt_raw, cwx, cbx, cwb, cbb, dt_bias, a_log)

    chunk4 = lambda b, c: (b, c, 0, 0)
    y = pl.pallas_call(
        _ssd_fwd_kernel,
        grid=(bsz, nc),
        in_specs=[
            pl.BlockSpec((1, L, SSD_INNER), lambda b, c: (b, c, REST_Z // SSD_INNER)),
            pl.BlockSpec((1, 1, SSD_INNER, L), chunk4),
            pl.BlockSpec((1, L, SSD_BC), lambda b, c: (b, c, 0)),
            pl.BlockSpec((1, L, 2 * SSD_HEADS), lambda b, c: (b, c, 0)),
            pl.BlockSpec((1, 2 * SSD_HEADS), const),
            pl.BlockSpec((1, 2 * SSD_HEADS), const),
            pl.BlockSpec(memory_space=pltpu.SMEM),
            pl.BlockSpec((1, SSD_INNER), const),
            pl.BlockSpec((1, 1, SSD_INNER, SSD_STATE), chunk4),
        ],
        out_specs=pl.BlockSpec((1, L, SSD_INNER), lambda b, c: (b, c, 0)),
        out_shape=jax.ShapeDtypeStruct((bsz, s, SSD_INNER), BF16),
        scratch_shapes=state_scratch + decay_scratch + [
            pltpu.VMEM((2 * SSD_HEADS, L), F32),
            pltpu.VMEM((2 * SSD_HEADS, L), F32),
            pltpu.VMEM((2 * SSD_HEADS, L), F32),
            pltpu.VMEM((2 * SSD_HEADS, L), F32),
            pltpu.VMEM((SSD_HEADS, L, L), F32),
            pltpu.VMEM((2, SSD_INNER, L), F32),
            pltpu.VMEM((SSD_INNER, L), F32),
        ],
        compiler_params=_cparams(("parallel", "arbitrary")),
        name="ssd_fwd",
    )(rest, xt, bc, dt_raw, dt_bias, a_log, d_skip, norm_w, states_b)
    return y


MIX_TN = 512


def _attn_merge_kernel(o0, o1, o2, l0, l1, l2, a_ref):
    ls = [l0[...], l1[...], l2[...]]
    mx = jnp.maximum(jnp.maximum(ls[0], ls[1]), ls[2])
    es = [jnp.exp(l - mx) for l in ls]
    den = es[0] + es[1] + es[2]
    num = es[0] * o0[...] + es[1] * o1[...] + es[2] * o2[...]
    a_ref[...] = (num / den).astype(BF16)


def _attn_merge(attn_outs, attn_lses):
    m = attn_outs[0].shape[0]
    tm = 512
    spec = pl.BlockSpec((tm, ATTN_OUT_WIDTH), lambda i: (i, 0))
    return pl.pallas_call(
        _attn_merge_kernel,
        grid=(m // tm,),
        in_specs=[spec] * 6,
        out_specs=spec,
        out_shape=jax.ShapeDtypeStruct((m, ATTN_OUT_WIDTH), BF16),
        compiler_params=_cparams(("parallel",)),
        name="attn_merge",
    )(*attn_outs, *attn_lses)


def _mix_kernel(a_ref, y_ref, ga_ref, gs_ref, wa_ref, ws_ref, m_ref):
    a_br = jnp.dot(a_ref[...], wa_ref[...], preferred_element_type=F32)
    s_br = jnp.dot(y_ref[...], ws_ref[...], preferred_element_type=F32)
    merged = jax.nn.sigmoid(ga_ref[...].astype(F32)) * a_br + jax.nn.sigmoid(gs_ref[...].astype(F32)) * s_br
    m_ref[...] = merged.astype(BF16)


def _mix(attn, y2d, rest2d, w_attn_proj, w_ssd_proj):
    m = y2d.shape[0]
    ga0 = REST_GA // MIX_TN
    gs0 = REST_GS // MIX_TN
    return pl.pallas_call(
        _mix_kernel,
        grid=(m // TM, D_MODEL // MIX_TN),
        in_specs=[
            pl.BlockSpec((TM, ATTN_OUT_WIDTH), lambda i, j: (i, 0)),
            pl.BlockSpec((TM, SSD_INNER), lambda i, j: (i, 0)),
            pl.BlockSpec((TM, MIX_TN), lambda i, j: (i, ga0 + j)),
            pl.BlockSpec((TM, MIX_TN), lambda i, j: (i, gs0 + j)),
            pl.BlockSpec((ATTN_OUT_WIDTH, MIX_TN), lambda i, j: (0, j)),
            pl.BlockSpec((SSD_INNER, MIX_TN), lambda i, j: (0, j)),
        ],
        out_specs=pl.BlockSpec((TM, MIX_TN), lambda i, j: (i, j)),
        out_shape=jax.ShapeDtypeStruct((m, D_MODEL), BF16),
        compiler_params=_cparams(("parallel", "arbitrary")),
        name="mix",
    )(attn, y2d, rest2d, rest2d, w_attn_proj, w_ssd_proj)


def _out_kernel(m_ref, w_ref, x_ref, mod_ref, nw_ref, x1_ref, h2_ref):
    mix = jnp.dot(m_ref[0], w_ref[...], preferred_element_type=F32)
    x1 = x_ref[0] + mod_ref[0, 2:3, :] * mix
    x1_ref[0] = x1
    h2_ref[0] = _rms_mod(x1, nw_ref[...], mod_ref[0, 4:5, :], mod_ref[0, 3:4, :]).astype(BF16)


def _out_proj(merged, w_out, x, mod, norm_w):
    bsz, s, _ = x.shape
    ts = 256
    tok = pl.BlockSpec((1, ts, D_MODEL), lambda b, i: (b, i, 0))
    return pl.pallas_call(
        _out_kernel,
        grid=(bsz, s // ts),
        in_specs=[tok,
                  pl.BlockSpec((D_MODEL, D_MODEL), lambda b, i: (0, 0)),
                  tok,
                  pl.BlockSpec((1, 6, D_MODEL), lambda b, i: (b, 0, 0)),
                  pl.BlockSpec((1, D_MODEL), lambda b, i: (0, 0))],
        out_specs=[tok, tok],
        out_shape=[jax.ShapeDtypeStruct((bsz, s, D_MODEL), F32),
                   jax.ShapeDtypeStruct((bsz, s, D_MODEL), BF16)],
        compiler_params=_cparams(("parallel", "parallel")),
        name="out_proj",
    )(merged, w_out, x, mod, norm_w.reshape(1, -1))


UP_TN = 2816


def _up_kernel(h_ref, w_ref, o_ref):
    o_ref[...] = jnp.dot(h_ref[...], w_ref[...], preferred_element_type=F32).astype(BF16)


def _up_proj(h2d, w_up):
    m = h2d.shape[0]
    return pl.pallas_call(
        _up_kernel,
        grid=(m // TM, 2 * D_FF // UP_TN),
        in_specs=[pl.BlockSpec((TM, D_MODEL), lambda i, j: (i, 0)),
                  pl.BlockSpec((D_MODEL, UP_TN), lambda i, j: (0, j))],
        out_specs=pl.BlockSpec((TM, UP_TN), lambda i, j: (i, j)),
        out_shape=jax.ShapeDtypeStruct((m, 2 * D_FF), BF16),
        compiler_params=_cparams(("parallel", "arbitrary")),
        name="ffn_up",
    )(h2d, w_up)


DOWN_TM = 512
DOWN_TK = 1408


def _down_kernel(g_ref, gp_ref, gn_ref, v_ref, cw_ref, cb_ref, w_ref, x1_ref, mod_ref, nw_ref, o_ref,
                 acc_ref, *, tiles_per_seq):
    i = pl.program_id(0)
    k = pl.program_id(1)
    tm = DOWN_TM

    @pl.when(k == 0)
    def _():
        acc_ref[...] = jnp.zeros_like(acc_ref)

    keep_prev = jnp.where(i % tiles_per_seq > 0, 1.0, 0.0).astype(F32)
    keep_next = jnp.where(i % tiles_per_seq < tiles_per_seq - 1, 1.0, 0.0).astype(F32)
    ext = _with_halo(g_ref[...], gp_ref[...], gn_ref[...], keep_prev, keep_next)
    gate = cb_ref[0] + ext[SUBLANES:SUBLANES + tm] * cw_ref[0, FFN_CONV // 2:FFN_CONV // 2 + 1, :]
    for j in range(FFN_CONV):
        if j != FFN_CONV // 2:
            gate = gate + _shifted_rows(ext, j - FFN_CONV // 2, tm) * cw_ref[0, j:j + 1, :]
    act = 0.5 * gate * (1.0 + lax.erf(gate * (2.0 ** -0.5))) * v_ref[...].astype(F32)
    acc_ref[...] += jnp.dot(act.astype(BF16), w_ref[...], preferred_element_type=F32)

    @pl.when(k == pl.num_programs(1) - 1)
    def _():
        x2 = x1_ref[...] + mod_ref[0, 5:6, :] * acc_ref[...]
        o_ref[...] = x2 * lax.rsqrt(jnp.mean(x2 * x2, axis=-1, keepdims=True) + EPS) * nw_ref[...]


def _down_proj(up2d, conv_w, conv_b, w_down, x1_2d, mod, norm_w, s):
    m = up2d.shape[0]
    tm, tk = DOWN_TM, DOWN_TK
    nk = D_FF // tk
    tiles_per_seq = s // tm
    hb = tm // HALO
    n_halo = m // HALO
    cw = conv_w.reshape(FFN_CONV, nk, tk).transpose(1, 0, 2)
    cb = conv_b.reshape(nk, 1, tk)
    return pl.pallas_call(
        functools.partial(_down_kernel, tiles_per_seq=tiles_per_seq),
        grid=(m // tm, nk),
        in_specs=[pl.BlockSpec((tm, tk), lambda i, k: (i, k)),
                  pl.BlockSpec((HALO, tk), lambda i, k: (jnp.maximum(i * hb - 1, 0), k)),
                  pl.BlockSpec((HALO, tk), lambda i, k: (jnp.minimum((i + 1) * hb, n_halo - 1), k)),
                  pl.BlockSpec((tm, tk), lambda i, k: (i, nk + k)),
                  pl.BlockSpec((1, FFN_CONV, tk), lambda i, k: (k, 0, 0)),
                  pl.BlockSpec((1, 1, tk), lambda i, k: (k, 0, 0)),
                  pl.BlockSpec((tk, D_MODEL), lambda i, k: (k, 0)),
                  pl.BlockSpec((tm, D_MODEL), lambda i, k: (i, 0)),
                  pl.BlockSpec((1, 6, D_MODEL), lambda i, k: (i // tiles_per_seq, 0, 0)),
                  pl.BlockSpec((1, D_MODEL), lambda i, k: (0, 0))],
        out_specs=pl.BlockSpec((tm, D_MODEL), lambda i, k: (i, 0)),
        out_shape=jax.ShapeDtypeStruct((m, D_MODEL), F32),
        scratch_shapes=[pltpu.VMEM((tm, D_MODEL), F32)],
        compiler_params=_cparams(("parallel", "arbitrary")),
        name="ffn_down",
    )(up2d, up2d, up2d, up2d, cw, cb, w_down, x1_2d, mod, norm_w.reshape(1, -1))


def _rotary_tables(s):
    pos = jnp.arange(s, dtype=F32)
    inv_freq = ROPE_THETA ** (-jnp.arange(0, HEAD_DIM, 2, dtype=F32) / HEAD_DIM)
    ang = pos[:, None] * inv_freq[None, :]
    cos, sin = jnp.cos(ang), jnp.sin(ang)
    return jnp.concatenate([cos, cos], axis=-1), jnp.concatenate([-sin, sin], axis=-1)


def _prep_weights(w_in, w_attn_proj, w_ssd_proj, w_out, w_up, w_down):
    o = np.cumsum((0, ATTN_WIDTH, ATTN_WIDTH, ATTN_WIDTH, SSD_INNER, SSD_INNER + SSD_BC, 2 * SSD_HEADS,
                   D_MODEL, D_MODEL))
    wq, wk, wv = (w_in[:, o[t]:o[t + 1]].reshape(D_MODEL, N_ATTN_HEADS, HEAD_DIM) for t in range(3))
    w_qkv = jnp.stack([wq, wk, wv], axis=2).reshape(D_MODEL, QKV_WIDTH)
    w_rest = jnp.concatenate([w_in[:, o[3]:o[5]], w_in[:, o[6]:o[8]]], axis=1)
    w_dt = w_in[:, o[5]:o[6]]
    cast = lambda w: w.astype(BF16)
    return (cast(w_qkv), cast(w_rest), cast(w_dt), cast(w_attn_proj), cast(w_ssd_proj), cast(w_out),
            cast(w_up), cast(w_down))


def _trunk(x, mod, wts, small):
    bsz, s, _ = x.shape
    m = bsz * s
    w_qkv, w_rest, w_dt, w_attn_proj, w_ssd_proj, w_out, w_up, w_down = wts
    (norm_mix_w, ssd_conv_w, ssd_conv_b, dt_bias, a_log, d_skip, ssd_norm_w, norm_ffn_w, ffn_conv_w, ffn_conv_b,
     norm_f_w) = small
    h = _norm_mod(x, mod, norm_mix_w)
    h2d = h.reshape(m, D_MODEL)
    cos_t, sin_t = _rotary_tables(s)
    rest2d, dt_raw = _rest_proj(h2d, w_rest, w_dt)
    outs, lses = [], []
    for gi, (_, dilation) in enumerate(ATTN_GROUPS):
        o, l = _attention_group(_qkv_proj(h, w_qkv, cos_t, sin_t, gi, dilation), dilation)
        outs.append(o.reshape(m, ATTN_OUT_WIDTH))
        lses.append(l.reshape(m, ATTN_OUT_WIDTH))
    y = _ssd(rest2d.reshape(bsz, s, REST_WIDTH), dt_raw.reshape(bsz, s, 2 * SSD_HEADS), ssd_conv_w, ssd_conv_b,
             dt_bias, a_log, d_skip, ssd_norm_w)
    merged = _mix(_attn_merge(outs, lses), y.reshape(m, SSD_INNER), rest2d, w_attn_proj, w_ssd_proj)
    x1, h2 = _out_proj(merged.reshape(bsz, s, D_MODEL), w_out, x, mod, norm_ffn_w)
    up = _up_proj(h2.reshape(m, D_MODEL), w_up)
    out = _down_proj(up, ffn_conv_w, ffn_conv_b, w_down, x1.reshape(m, D_MODEL), mod, norm_f_w, s)
    return out.reshape(bsz, s, D_MODEL)


def kernel(x_prompt, x_sample, c_prompt, c_sample, w_ada, b_ada, norm_mix_w, w_in, ssd_conv_w, ssd_conv_b,
           dt_bias_fwd, dt_bias_bwd, a_log_fwd, a_log_bwd, ssd_d, ssd_norm_w, w_attn_proj, w_ssd_proj, w_out,
           norm_ffn_w, w_up, ffn_conv_w, ffn_conv_b, w_down, norm_f_w):
    wts = _prep_weights(w_in[0], w_attn_proj[0], w_ssd_proj[0], w_out[0], w_up[0], w_down[0])
    small = (norm_mix_w[0], ssd_conv_w[0], ssd_conv_b[0].reshape(1, -1),
             jnp.concatenate([dt_bias_fwd[0], dt_bias_bwd[0]]).reshape(1, -1),
             jnp.concatenate([a_log_fwd[0], a_log_bwd[0]]).reshape(1, -1),
             ssd_d[0], ssd_norm_w[0].reshape(1, -1),
             norm_ffn_w[0], ffn_conv_w[0], ffn_conv_b[0], norm_f_w)
    nb_p = c_prompt.shape[0]
    nb_s = c_sample.shape[0]
    pad = (-(nb_p + nb_s)) % 8
    c_all = jnp.concatenate([c_prompt, c_sample, jnp.zeros((pad, D_MODEL), F32)], axis=0)
    mod_all = _modulation(c_all, w_ada[0], b_ada[0]).reshape(-1, 6, D_MODEL)
    y_prompt = _trunk(x_prompt, mod_all[:nb_p], wts, small)
    y_sample = _trunk(x_sample, mod_all[nb_p:nb_p + nb_s], wts, small)
    return (y_prompt, y_sample)
```

```python
import functools

import numpy as np
import jax
import jax.numpy as jnp
from jax import lax
from jax.experimental import pallas as pl
from jax.experimental.pallas import tpu as pltpu

F32 = jnp.float32
BF16 = jnp.bfloat16

D_MODEL = 2048
HEAD_DIM = 128
ATTN_GROUPS = ((128, 1), (512, 4), (2048, 16))
HEADS_PER_GROUP = 4
N_ATTN_HEADS = HEADS_PER_GROUP * len(ATTN_GROUPS)
ATTN_WIDTH = N_ATTN_HEADS * HEAD_DIM
ATTN_OUT_WIDTH = HEADS_PER_GROUP * HEAD_DIM
ROPE_THETA = 10000.0
SSD_INNER = 2 * D_MODEL
SSD_HEAD_DIM = 64
SSD_HEADS = SSD_INNER // SSD_HEAD_DIM
SSD_GROUPS = 8
SSD_HEADS_PER_GROUP = SSD_HEADS // SSD_GROUPS
SSD_STATE = 128
SSD_CONV = 5
SSD_CHUNK = 128
SSD_BC = 2 * SSD_GROUPS * SSD_STATE
D_FF = 5632
FFN_CONV = 3
EPS = 1e-6
NEG_INF = -1e30
LOG2E = 1.4426950408889634

QKV_WIDTH = 3 * ATTN_WIDTH
REST_WIDTH = 2 * SSD_INNER + SSD_BC + 2 * D_MODEL
REST_Z, REST_XS, REST_BC, REST_GA, REST_GS = 0, SSD_INNER, 2 * SSD_INNER, 2 * SSD_INNER + SSD_BC, 2 * SSD_INNER + SSD_BC + D_MODEL

LANES = 128
SUBLANES = 8
HALO = 16
VMEM_LIMIT = 56 * 1024 * 1024
TM = 1024


def _cparams(sem):
    return pltpu.CompilerParams(dimension_semantics=sem, vmem_limit_bytes=VMEM_LIMIT)


def _silu(x):
    h = 0.5 * x
    return h + h * jnp.tanh(h)


def _shifted_rows(ext, shift, rows):
    if shift == 0:
        return ext[SUBLANES:SUBLANES + rows]
    return pltpu.roll(ext, (-shift) % ext.shape[0], 0)[SUBLANES:SUBLANES + rows]


def _with_halo(main_ref_val, prev_blk, next_blk, keep_prev, keep_next):
    prev = prev_blk.astype(F32)[HALO - SUBLANES:HALO] * keep_prev
    nxt = next_blk.astype(F32)[0:SUBLANES] * keep_next
    return jnp.concatenate([prev, main_ref_val.astype(F32), nxt], axis=0)


def _mod_kernel(c_ref, w_ref, b_ref, o_ref):
    c = c_ref[...]
    o_ref[...] = jnp.dot(c * jax.nn.sigmoid(c), w_ref[...], preferred_element_type=F32,
                         precision=lax.Precision.HIGHEST) + b_ref[...]


def _modulation(c_all, w_ada, b_ada):
    rows = c_all.shape[0]
    tn = 1024
    return pl.pallas_call(
        _mod_kernel,
        grid=(6 * D_MODEL // tn,),
        in_specs=[pl.BlockSpec((rows, D_MODEL), lambda j: (0, 0)),
                  pl.BlockSpec((D_MODEL, tn), lambda j: (0, j)),
                  pl.BlockSpec((1, tn), lambda j: (0, j))],
        out_specs=pl.BlockSpec((rows, tn), lambda j: (0, j)),
        out_shape=jax.ShapeDtypeStruct((rows, 6 * D_MODEL), F32),
        compiler_params=_cparams(("arbitrary",)),
        name="mod",
    )(c_all, w_ada, b_ada.reshape(1, -1))


def _rms_mod(x, w, scale, shift):
    y = x * lax.rsqrt(jnp.mean(x * x, axis=-1, keepdims=True) + EPS) * w
    return y * (1.0 + scale) + shift


def _norm_kernel(x_ref, mod_ref, w_ref, h_ref):
    h_ref[0] = _rms_mod(x_ref[0], w_ref[...], mod_ref[0, 1:2, :], mod_ref[0, 0:1, :]).astype(BF16)


def _norm_mod(x, mod, w):
    bsz, s, _ = x.shape
    ts = 512
    return pl.pallas_call(
        _norm_kernel,
        grid=(bsz, s // ts),
        in_specs=[pl.BlockSpec((1, ts, D_MODEL), lambda b, i: (b, i, 0)),
                  pl.BlockSpec((1, 6, D_MODEL), lambda b, i: (b, 0, 0)),
                  pl.BlockSpec((1, D_MODEL), lambda b, i: (0, 0))],
        out_specs=pl.BlockSpec((1, ts, D_MODEL), lambda b, i: (b, i, 0)),
        out_shape=jax.ShapeDtypeStruct((bsz, s, D_MODEL), BF16),
        compiler_params=_cparams(("parallel", "parallel")),
        name="norm_mix",
    )(x, mod, w.reshape(1, -1))


QKV_GROUP_WIDTH = 3 * ATTN_OUT_WIDTH
QKV_TN = QKV_GROUP_WIDTH


def _qkv_kernel(h_ref, w_ref, cos_ref, sin_ref, o_ref, *scratch, dilation):
    acc = jnp.dot(h_ref[0], w_ref[...], preferred_element_type=F32)
    cos = cos_ref[...]
    sin = sin_ref[...]
    tiles = QKV_TN // HEAD_DIM
    for t in range(tiles):
        sl = slice(t * HEAD_DIM, (t + 1) * HEAD_DIM)
        a = acc[:, sl]
        if t % 3 != 2:
            a = a * cos + pltpu.roll(a, HEAD_DIM // 2, 1) * sin
        if dilation == 1:
            o_ref[0, 0, :, sl] = a.astype(BF16)
        else:
            scratch[0][t] = a
    if dilation > 1:
        rows = TM // dilation
        for r in range(dilation):
            for t in range(tiles):
                sl = slice(t * HEAD_DIM, (t + 1) * HEAD_DIM)
                o_ref[0, r, :, sl] = scratch[0][t, pl.ds(r, rows, stride=dilation), :].astype(BF16)


def _qkv_proj(h, w_qkv, cos_t, sin_t, gi, dilation):
    bsz, s, _ = h.shape
    col_tiles = QKV_GROUP_WIDTH // QKV_TN
    rows = TM // dilation
    return pl.pallas_call(
        functools.partial(_qkv_kernel, dilation=dilation),
        grid=(bsz, s // TM, col_tiles),
        in_specs=[pl.BlockSpec((1, TM, D_MODEL), lambda b, i, j: (b, i, 0)),
                  pl.BlockSpec((D_MODEL, QKV_TN), lambda b, i, j: (0, gi * col_tiles + j)),
                  pl.BlockSpec((TM, HEAD_DIM), lambda b, i, j: (i, 0)),
                  pl.BlockSpec((TM, HEAD_DIM), lambda b, i, j: (i, 0))],
        out_specs=pl.BlockSpec((1, dilation, rows, QKV_TN), lambda b, i, j: (b, 0, i, j)),
        out_shape=jax.ShapeDtypeStruct((bsz, dilation, s // dilation, QKV_GROUP_WIDTH), BF16),
        scratch_shapes=[pltpu.VMEM((QKV_TN // HEAD_DIM, TM, HEAD_DIM), F32)] if dilation > 1 else [],
        compiler_params=_cparams(("parallel", "parallel", "arbitrary")),
        name=f"qkv_proj_d{dilation}",
    )(h, w_qkv, cos_t, sin_t)


REST_TN = 2048


def _rest_kernel(h_ref, w_ref, wdt_ref, o_ref, dt_ref):
    o_ref[...] = jnp.dot(h_ref[...], w_ref[...], preferred_element_type=F32).astype(BF16)

    @pl.when(pl.program_id(1) == 0)
    def _():
        dt_ref[...] = jnp.dot(h_ref[...], wdt_ref[...], preferred_element_type=F32)


def _rest_proj(h2d, w_rest, w_dt):
    m = h2d.shape[0]
    return pl.pallas_call(
        _rest_kernel,
        grid=(m // TM, REST_WIDTH // REST_TN),
        in_specs=[pl.BlockSpec((TM, D_MODEL), lambda i, j: (i, 0)),
                  pl.BlockSpec((D_MODEL, REST_TN), lambda i, j: (0, j)),
                  pl.BlockSpec((D_MODEL, 2 * SSD_HEADS), lambda i, j: (0, 0))],
        out_specs=[pl.BlockSpec((TM, REST_TN), lambda i, j: (i, j)),
                   pl.BlockSpec((TM, 2 * SSD_HEADS), lambda i, j: (i, 0))],
        out_shape=[jax.ShapeDtypeStruct((m, REST_WIDTH), BF16),
                   jax.ShapeDtypeStruct((m, 2 * SSD_HEADS), F32)],
        compiler_params=_cparams(("parallel", "arbitrary")),
        name="rest_proj",
    )(h2d, w_rest, w_dt)


ATTN_TQ = 128
ATTN_RADIUS = 64
ATTN_UNROLL = 4


def _attn_kernel(qkv_ref, o_ref, l_ref, *, n, dilation):
    tq = ATTN_TQ
    nq = n // tq
    win = min(n, tq + 2 * ATTN_RADIUS)
    scale = HEAD_DIM ** -0.5

    def block(it):
        r = it // nq
        i0 = pl.multiple_of((it % nq) * tq, tq)
        start = pl.multiple_of(jnp.clip(i0 - ATTN_RADIUS, 0, n - win), ATTN_RADIUS)
        rows = i0 + lax.broadcasted_iota(jnp.int32, (tq, win), 0)
        cols = start + lax.broadcasted_iota(jnp.int32, (tq, win), 1)
        valid = jnp.abs(cols - rows) <= ATTN_RADIUS
        q = qkv_ref[0, r, pl.ds(i0, tq), 0:HEAD_DIM]
        k = qkv_ref[0, r, pl.ds(start, win), HEAD_DIM:2 * HEAD_DIM]
        v = qkv_ref[0, r, pl.ds(start, win), 2 * HEAD_DIM:3 * HEAD_DIM]
        sc = lax.dot_general(q, k, (((1,), (1,)), ((), ())), preferred_element_type=F32) * scale
        sc = jnp.where(valid, sc, NEG_INF)
        mx = jnp.max(sc, axis=-1, keepdims=True)
        p = jnp.exp(sc - mx)
        den = jnp.sum(p, axis=-1, keepdims=True)
        o = jnp.dot(p.astype(BF16), v, preferred_element_type=F32) / den
        lse = jnp.broadcast_to(mx + jnp.log(den), (tq, HEAD_DIM))
        if dilation == 1:
            dst = pl.ds(i0, tq)
        else:
            dst = pl.ds(i0 * dilation + r, tq, stride=dilation)
        o_ref[0, dst, :] = o
        l_ref[0, dst, :] = lse

    def body(trip, carry):
        for u in range(ATTN_UNROLL):
            block(trip * ATTN_UNROLL + u)
        return carry

    lax.fori_loop(0, dilation * nq // ATTN_UNROLL, body, 0)


def _attention_group(qkv_g, dilation):
    bsz, _, n, _ = qkv_g.shape
    s = n * dilation
    assert (dilation * (n // ATTN_TQ)) % ATTN_UNROLL == 0
    ospec = pl.BlockSpec((1, s, HEAD_DIM), lambda b, hh: (b, 0, hh))
    return pl.pallas_call(
        functools.partial(_attn_kernel, n=n, dilation=dilation),
        grid=(bsz, HEADS_PER_GROUP),
        in_specs=[pl.BlockSpec((1, dilation, n, 3 * HEAD_DIM), lambda b, hh: (b, 0, 0, hh))],
        out_specs=[ospec, ospec],
        out_shape=[jax.ShapeDtypeStruct((bsz, s, ATTN_OUT_WIDTH), F32)] * 2,
        compiler_params=_cparams(("parallel", "parallel")),
        name=f"attn_d{dilation}",
    )(qkv_g)


SSD_CHUNKS_PER_STEP = 2


def _ssd_decay_terms(dt_raw, dtb_ref, alog_ref):
    L = SSD_CHUNK
    dt = jax.nn.softplus(dt_raw + dtb_ref[...])
    dta = dt * (-jnp.exp(alog_ref[...]))
    row = lax.broadcasted_iota(jnp.int32, (L, L), 0)
    col = lax.broadcasted_iota(jnp.int32, (L, L), 1)
    cum_f = jnp.dot((col <= row).astype(F32), dta, preferred_element_type=F32, precision=lax.Precision.HIGHEST)
    cum_b = jnp.dot((col >= row).astype(F32), dta, preferred_element_type=F32, precision=lax.Precision.HIGHEST)
    lane = lax.broadcasted_iota(jnp.int32, (L, 2 * SSD_HEADS), 1)
    acum_t = jnp.where(lane < SSD_HEADS, cum_f, cum_b).T
    dt_t = dt.T
    hrow = lax.broadcasted_iota(jnp.int32, (2 * SSD_HEADS, 1), 0)
    tot = jnp.where(hrow < SSD_HEADS, acum_t[:, L - 1:L], acum_t[:, 0:1])
    return acum_t, dt_t, tot


def _bcast_row(ref, r, rows):
    return jnp.broadcast_to(ref[pl.ds(r, 1), :], (rows, ref.shape[1]))


def _ssd_bwd_kernel(xs_ref, xsp_ref, xsn_ref, bc_ref, bcp_ref, bcn_ref, dt_ref, cwx_ref, cbx_ref, cwb_ref, cbb_ref,
                    dtb_ref, alog_ref, sout_ref, xt_out_ref, bc_out_ref, state_ref, xt_ref, dec_ref, tot_ref):
    L, P, N, R = SSD_CHUNK, SSD_HEAD_DIM, SSD_STATE, SSD_HEADS_PER_GROUP
    gw = R * P
    c = pl.program_id(1)
    nb = pl.num_programs(1)
    cc = nb - 1 - c

    @pl.when(c == 0)
    def _():
        state_ref[...] = jnp.zeros_like(state_ref)

    keep_prev = jnp.where(cc > 0, 1.0, 0.0).astype(F32)
    keep_next = jnp.where(cc < nb - 1, 1.0, 0.0).astype(F32)

    def conv_silu(main_ref, prev_ref, next_ref, w_ref, b_ref, col0):
        sl = slice(col0, col0 + gw)
        ext = _with_halo(main_ref[0, :, sl], prev_ref[0, :, sl], next_ref[0, :, sl], keep_prev, keep_next)
        outs = []
        for sub in range(SSD_CHUNKS_PER_STEP):
            e = ext[sub * L:sub * L + L + 2 * SUBLANES]
            acc = b_ref[:, sl] + e[SUBLANES:SUBLANES + L] * w_ref[SSD_CONV // 2:SSD_CONV // 2 + 1, sl]
            for j in range(SSD_CONV):
                if j != SSD_CONV // 2:
                    acc = acc + _shifted_rows(e, j - SSD_CONV // 2, L) * w_ref[j:j + 1, sl]
            outs.append(_silu(acc))
        return outs

    for g in range(SSD_GROUPS):
        for sub, xg in enumerate(conv_silu(xs_ref, xsp_ref, xsn_ref, cwx_ref, cbx_ref, g * gw)):
            for q in range(gw // LANES):
                rows = slice(g * gw + q * LANES, g * gw + (q + 1) * LANES)
                x_t = xg[:, q * LANES:(q + 1) * LANES].T
                xt_ref[sub, rows, :] = x_t
                xt_out_ref[0, sub, rows, :] = x_t.astype(BF16)
    for q in range(SSD_BC // gw):
        for sub, bcg in enumerate(conv_silu(bc_ref, bcp_ref, bcn_ref, cwb_ref, cbb_ref, q * gw)):
            bc_out_ref[0, sub * L:(sub + 1) * L, q * gw:(q + 1) * gw] = bcg.astype(BF16)

    for sub in reversed(range(SSD_CHUNKS_PER_STEP)):
        trows = slice(sub * L, (sub + 1) * L)
        acum_t, dt_t, tot = _ssd_decay_terms(dt_ref[0, trows, :], dtb_ref, alog_ref)
        dec_ref[...] = jnp.exp(tot - acum_t) * dt_t
        tot_ref[...] = jnp.broadcast_to(jnp.exp(tot), (2 * SSD_HEADS, N))
        sout_ref[0, sub] = state_ref[...].astype(BF16)
        for g in range(SSD_GROUPS):
            b_g = bc_out_ref[0, trows, g * N:(g + 1) * N]
            for r in range(R):
                h = g * R + r
                hrows = slice(h * P, (h + 1) * P)
                hd = h + SSD_HEADS
                xsc = (xt_ref[sub, hrows, :] * _bcast_row(dec_ref, hd, P)).astype(BF16)
                state_ref[hrows, :] = (state_ref[hrows, :] * _bcast_row(tot_ref, hd, P)
                                       + jnp.dot(xsc, b_g, preferred_element_type=F32))


def _ssd_fwd_kernel(z_ref, xt_ref, bc_ref, dt_ref, dtb_ref, alog_ref, dskip_ref, nw_ref, sb_ref, y_ref,
                    state_ref, dec_ref, tot_ref, at_ref, ar_ref, dtt_ref, e_ref, acol_ref, ys_ref, yt_ref):
    L, P, N, R = SSD_CHUNK, SSD_HEAD_DIM, SSD_STATE, SSD_HEADS_PER_GROUP
    gw = R * P
    nt = (((1,), (1,)), ((), ()))

    @pl.when(pl.program_id(1) == 0)
    def _():
        state_ref[...] = jnp.zeros_like(state_ref)

    row = lax.broadcasted_iota(jnp.int32, (L, L), 0)
    col = lax.broadcasted_iota(jnp.int32, (L, L), 1)

    def build_acol(g):
        for r in range(R):
            hf = g * R + r
            u = jnp.where(row <= col, _bcast_row(at_ref, hf, L), _bcast_row(at_ref, hf + SSD_HEADS, L))
            acol_ref[hf] = u.T

    for sub in range(SSD_CHUNKS_PER_STEP):
        trows = slice(sub * L, (sub + 1) * L)
        acum_t, dt_t, tot = _ssd_decay_terms(dt_ref[0, trows, :], dtb_ref, alog_ref)
        dec_ref[...] = jnp.exp(tot - acum_t) * dt_t
        tot_ref[...] = jnp.broadcast_to(jnp.exp(tot), (2 * SSD_HEADS, N))
        at_ref[...] = acum_t * LOG2E
        ar_ref[...] = (acum_t - jnp.log(dt_t)) * LOG2E
        dtt_ref[...] = dt_t
        e_ref[...] = jnp.exp(acum_t)
        for g in range(SSD_GROUPS):
            build_acol(g)
        for g in range(SSD_GROUPS):
            b_g = bc_ref[0, trows, g * N:(g + 1) * N]
            c_g = bc_ref[0, trows, (SSD_GROUPS + g) * N:(SSD_GROUPS + g + 1) * N]
            grows = slice(g * gw, (g + 1) * gw)
            cb = lax.dot_general(c_g, b_g, nt, preferred_element_type=F32)
            cb_diag = jnp.sum(jnp.where(row == col, cb, 0.0), axis=0, keepdims=True)
            ys_ref[0, grows, :] = lax.dot_general(state_ref[grows, :].astype(BF16), c_g, nt, preferred_element_type=F32)
            ys_ref[1, grows, :] = lax.dot_general(sb_ref[0, sub, grows, :], c_g, nt, preferred_element_type=F32)
            for r in range(R):
                h = g * R + r
                hf = h
                hb = h + SSD_HEADS
                hrows = slice(h * P, (h + 1) * P)
                x_t = xt_ref[0, sub, hrows, :]
                x_f = x_t.astype(F32)
                arg = acol_ref[h] - jnp.where(row >= col, _bcast_row(ar_ref, hf, L), _bcast_row(ar_ref, hb, L))
                w = (cb * jnp.exp2(arg)).astype(BF16)
                y_t = lax.dot_general(x_t, w, nt, preferred_element_type=F32)
                y_t = y_t + ys_ref[0, hrows, :] * _bcast_row(e_ref, hf, P)
                y_t = y_t + ys_ref[1, hrows, :] * _bcast_row(e_ref, hb, P)
                yt_ref[hrows, :] = y_t + x_f * (dskip_ref[h] + cb_diag * dtt_ref[hb:hb + 1, :])
                xsc = (x_f * _bcast_row(dec_ref, hf, P)).astype(BF16)
                state_ref[hrows, :] = (state_ref[hrows, :] * _bcast_row(tot_ref, hf, P)
                                       + jnp.dot(xsc, b_g, preferred_element_type=F32))
            ys = []
            ssq = None
            for q in range(gw // LANES):
                sl = slice(g * gw + q * LANES, g * gw + (q + 1) * LANES)
                yb = yt_ref[sl, :].T * _silu(z_ref[0, trows, sl].astype(F32))
                ys.append(yb)
                part = jnp.sum(yb * yb, axis=-1, keepdims=True)
                ssq = part if ssq is None else ssq + part
            inv = lax.rsqrt(ssq * (1.0 / gw) + EPS)
            for q, yb in enumerate(ys):
                sl = slice(g * gw + q * LANES, g * gw + (q + 1) * LANES)
                y_ref[0, trows, sl] = (yb * inv * nw_ref[:, sl]).astype(BF16)


def _ssd(rest, dt_raw, conv_w, conv_b, dt_bias, a_log, d_skip, norm_w):
    bsz, s, _ = rest.shape
    L = SSD_CHUNK
    cps = SSD_CHUNKS_PER_STEP
    lb = cps * L
    nc = s // L
    nb = s // lb
    hb = lb // HALO
    n_halo = s // HALO
    cwx, cwb = conv_w[:, :SSD_INNER], conv_w[:, SSD_INNER:]
    cbx, cbb = conv_b[:, :SSD_INNER], conv_b[:, SSD_INNER:]

    rev = lambda c: nb - 1 - c
    prev = lambda col: (lambda b, c: (b, jnp.maximum(rev(c) * hb - 1, 0), col))
    nxt = lambda col: (lambda b, c: (b, jnp.minimum((rev(c) + 1) * hb, n_halo - 1), col))
    const = lambda b, c: (0, 0)
    xs_col, bc_col = REST_XS // SSD_INNER, REST_BC // SSD_BC
    state_scratch = [
        pltpu.VMEM((SSD_INNER, SSD_STATE), F32),
    ]
    decay_scratch = [
        pltpu.VMEM((2 * SSD_HEADS, L), F32),
        pltpu.VMEM((2 * SSD_HEADS, SSD_STATE), F32),
    ]

    states_b, xt, bc = pl.pallas_call(
        _ssd_bwd_kernel,
        grid=(bsz, nb),
        in_specs=[
            pl.BlockSpec((1, lb, SSD_INNER), lambda b, c: (b, rev(c), xs_col)),
            pl.BlockSpec((1, HALO, SSD_INNER), prev(xs_col)),
            pl.BlockSpec((1, HALO, SSD_INNER), nxt(xs_col)),
            pl.BlockSpec((1, lb, SSD_BC), lambda b, c: (b, rev(c), bc_col)),
            pl.BlockSpec((1, HALO, SSD_BC), prev(bc_col)),
            pl.BlockSpec((1, HALO, SSD_BC), nxt(bc_col)),
            pl.BlockSpec((1, lb, 2 * SSD_HEADS), lambda b, c: (b, rev(c), 0)),
            pl.BlockSpec((SSD_CONV, SSD_INNER), const),
            pl.BlockSpec((1, SSD_INNER), const),
            pl.BlockSpec((SSD_CONV, SSD_BC), const),
            pl.BlockSpec((1, SSD_BC), const),
            pl.BlockSpec((1, 2 * SSD_HEADS), const),
            pl.BlockSpec((1, 2 * SSD_HEADS), const),
        ],
        out_specs=[pl.BlockSpec((1, cps, SSD_INNER, SSD_STATE), lambda b, c: (b, rev(c), 0, 0)),
                   pl.BlockSpec((1, cps, SSD_INNER, L), lambda b, c: (b, rev(c), 0, 0)),
                   pl.BlockSpec((1, lb, SSD_BC), lambda b, c: (b, rev(c), 0))],
        out_shape=[jax.ShapeDtypeStruct((bsz, nc, SSD_INNER, SSD_STATE), BF16),
                   jax.ShapeDtypeStruct((bsz, nc, SSD_INNER, L), BF16),
                   jax.ShapeDtypeStruct((bsz, s, SSD_BC), BF16)],
        scratch_shapes=state_scratch + [pltpu.VMEM((cps, SSD_INNER, L), F32)] + decay_scratch,
        compiler_params=_cparams(("parallel", "arbitrary")),
        name="ssd_bwd_states",
    )(rest, rest, rest, rest, rest, rest, dt_raw, cwx, cbx, cwb, cbb, dt_bias, a_log)

    chunk4 = lambda b, c: (b, c, 0, 0)
    y = pl.pallas_call(
        _ssd_fwd_kernel,
        grid=(bsz, nb),
        in_specs=[
            pl.BlockSpec((1, lb, SSD_INNER), lambda b, c: (b, c, REST_Z // SSD_INNER)),
            pl.BlockSpec((1, cps, SSD_INNER, L), chunk4),
            pl.BlockSpec((1, lb, SSD_BC), lambda b, c: (b, c, 0)),
            pl.BlockSpec((1, lb, 2 * SSD_HEADS), lambda b, c: (b, c, 0)),
            pl.BlockSpec((1, 2 * SSD_HEADS), const),
            pl.BlockSpec((1, 2 * SSD_HEADS), const),
            pl.BlockSpec(memory_space=pltpu.SMEM),
            pl.BlockSpec((1, SSD_INNER), const),
            pl.BlockSpec((1, cps, SSD_INNER, SSD_STATE), chunk4),
        ],
        out_specs=pl.BlockSpec((1, lb, SSD_INNER), lambda b, c: (b, c, 0)),
        out_shape=jax.ShapeDtypeStruct((bsz, s, SSD_INNER), BF16),
        scratch_shapes=state_scratch + decay_scratch + [
            pltpu.VMEM((2 * SSD_HEADS, L), F32),
            pltpu.VMEM((2 * SSD_HEADS, L), F32),
            pltpu.VMEM((2 * SSD_HEADS, L), F32),
            pltpu.VMEM((2 * SSD_HEADS, L), F32),
            pltpu.VMEM((SSD_HEADS, L, L), F32),
            pltpu.VMEM((2, SSD_INNER, L), F32),
            pltpu.VMEM((SSD_INNER, L), F32),
        ],
        compiler_params=_cparams(("parallel", "arbitrary")),
        name="ssd_fwd",
    )(rest, xt, bc, dt_raw, dt_bias, a_log, d_skip, norm_w, states_b)
    return y


MIX_TN = 512


def _attn_merge_kernel(o0, o1, o2, l0, l1, l2, a_ref):
    ls = [l0[...], l1[...], l2[...]]
    mx = jnp.maximum(jnp.maximum(ls[0], ls[1]), ls[2])
    es = [jnp.exp(l - mx) for l in ls]
    den = es[0] + es[1] + es[2]
    num = es[0] * o0[...] + es[1] * o1[...] + es[2] * o2[...]
    a_ref[...] = (num / den).astype(BF16)


def _attn_merge(attn_outs, attn_lses):
    m = attn_outs[0].shape[0]
    tm = 512
    spec = pl.BlockSpec((tm, ATTN_OUT_WIDTH), lambda i: (i, 0))
    return pl.pallas_call(
        _attn_merge_kernel,
        grid=(m // tm,),
        in_specs=[spec] * 6,
        out_specs=spec,
        out_shape=jax.ShapeDtypeStruct((m, ATTN_OUT_WIDTH), BF16),
        compiler_params=_cparams(("parallel",)),
        name="attn_merge",
    )(*attn_outs, *attn_lses)


def _mix_kernel(a_ref, y_ref, ga_ref, gs_ref, wa_ref, ws_ref, m_ref):
    a_br = jnp.dot(a_ref[...], wa_ref[...], preferred_element_type=F32)
    s_br = jnp.dot(y_ref[...], ws_ref[...], preferred_element_type=F32)
    merged = jax.nn.sigmoid(ga_ref[...].astype(F32)) * a_br + jax.nn.sigmoid(gs_ref[...].astype(F32)) * s_br
    m_ref[...] = merged.astype(BF16)


def _mix(attn, y2d, rest2d, w_attn_proj, w_ssd_proj):
    m = y2d.shape[0]
    ga0 = REST_GA // MIX_TN
    gs0 = REST_GS // MIX_TN
    return pl.pallas_call(
        _mix_kernel,
        grid=(m // TM, D_MODEL // MIX_TN),
        in_specs=[
            pl.BlockSpec((TM, ATTN_OUT_WIDTH), lambda i, j: (i, 0)),
            pl.BlockSpec((TM, SSD_INNER), lambda i, j: (i, 0)),
            pl.BlockSpec((TM, MIX_TN), lambda i, j: (i, ga0 + j)),
            pl.BlockSpec((TM, MIX_TN), lambda i, j: (i, gs0 + j)),
            pl.BlockSpec((ATTN_OUT_WIDTH, MIX_TN), lambda i, j: (0, j)),
            pl.BlockSpec((SSD_INNER, MIX_TN), lambda i, j: (0, j)),
        ],
        out_specs=pl.BlockSpec((TM, MIX_TN), lambda i, j: (i, j)),
        out_shape=jax.ShapeDtypeStruct((m, D_MODEL), BF16),
        compiler_params=_cparams(("parallel", "arbitrary")),
        name="mix",
    )(attn, y2d, rest2d, rest2d, w_attn_proj, w_ssd_proj)


def _out_kernel(m_ref, w_ref, x_ref, mod_ref, nw_ref, x1_ref, h2_ref):
    mix = jnp.dot(m_ref[0], w_ref[...], preferred_element_type=F32)
    x1 = x_ref[0] + mod_ref[0, 2:3, :] * mix
    x1_ref[0] = x1
    h2_ref[0] = _rms_mod(x1, nw_ref[...], mod_ref[0, 4:5, :], mod_ref[0, 3:4, :]).astype(BF16)


def _out_proj(merged, w_out, x, mod, norm_w):
    bsz, s, _ = x.shape
    ts = 512
    tok = pl.BlockSpec((1, ts, D_MODEL), lambda b, i: (b, i, 0))
    return pl.pallas_call(
        _out_kernel,
        grid=(bsz, s // ts),
        in_specs=[tok,
                  pl.BlockSpec((D_MODEL, D_MODEL), lambda b, i: (0, 0)),
                  tok,
                  pl.BlockSpec((1, 6, D_MODEL), lambda b, i: (b, 0, 0)),
                  pl.BlockSpec((1, D_MODEL), lambda b, i: (0, 0))],
        out_specs=[tok, tok],
        out_shape=[jax.ShapeDtypeStruct((bsz, s, D_MODEL), F32),
                   jax.ShapeDtypeStruct((bsz, s, D_MODEL), BF16)],
        compiler_params=_cparams(("parallel", "parallel")),
        name="out_proj",
    )(merged, w_out, x, mod, norm_w.reshape(1, -1))


UP_TN = 2816


def _up_kernel(h_ref, w_ref, o_ref):
    o_ref[...] = jnp.dot(h_ref[...], w_ref[...], preferred_element_type=F32).astype(BF16)


def _up_proj(h2d, w_up):
    m = h2d.shape[0]
    return pl.pallas_call(
        _up_kernel,
        grid=(m // TM, 2 * D_FF // UP_TN),
        in_specs=[pl.BlockSpec((TM, D_MODEL), lambda i, j: (i, 0)),
                  pl.BlockSpec((D_MODEL, UP_TN), lambda i, j: (0, j))],
        out_specs=pl.BlockSpec((TM, UP_TN), lambda i, j: (i, j)),
        out_shape=jax.ShapeDtypeStruct((m, 2 * D_FF), BF16),
        compiler_params=_cparams(("parallel", "arbitrary")),
        name="ffn_up",
    )(h2d, w_up)


DOWN_TM = 512
DOWN_TK = 1408


def _down_kernel(g_ref, gp_ref, gn_ref, v_ref, cw_ref, cb_ref, w_ref, x1_ref, mod_ref, nw_ref, o_ref,
                 acc_ref, *, tiles_per_seq):
    i = pl.program_id(0)
    k = pl.program_id(1)
    tm = DOWN_TM

    @pl.when(k == 0)
    def _():
        acc_ref[...] = jnp.zeros_like(acc_ref)

    keep_prev = jnp.where(i % tiles_per_seq > 0, 1.0, 0.0).astype(F32)
    keep_next = jnp.where(i % tiles_per_seq < tiles_per_seq - 1, 1.0, 0.0).astype(F32)
    ext = _with_halo(g_ref[...], gp_ref[...], gn_ref[...], keep_prev, keep_next)
    gate = cb_ref[0] + ext[SUBLANES:SUBLANES + tm] * cw_ref[0, FFN_CONV // 2:FFN_CONV // 2 + 1, :]
    for j in range(FFN_CONV):
        if j != FFN_CONV // 2:
            gate = gate + _shifted_rows(ext, j - FFN_CONV // 2, tm) * cw_ref[0, j:j + 1, :]
    act = 0.5 * gate * (1.0 + lax.erf(gate * (2.0 ** -0.5))) * v_ref[...].astype(F32)
    acc_ref[...] += jnp.dot(act.astype(BF16), w_ref[...], preferred_element_type=F32)

    @pl.when(k == pl.num_programs(1) - 1)
    def _():
        x2 = x1_ref[...] + mod_ref[0, 5:6, :] * acc_ref[...]
        o_ref[...] = x2 * lax.rsqrt(jnp.mean(x2 * x2, axis=-1, keepdims=True) + EPS) * nw_ref[...]


def _down_proj(up2d, conv_w, conv_b, w_down, x1_2d, mod, norm_w, s):
    m = up2d.shape[0]
    tm, tk = DOWN_TM, DOWN_TK
    nk = D_FF // tk
    tiles_per_seq = s // tm
    hb = tm // HALO
    n_halo = m // HALO
    cw = conv_w.reshape(FFN_CONV, nk, tk).transpose(1, 0, 2)
    cb = conv_b.reshape(nk, 1, tk)
    return pl.pallas_call(
        functools.partial(_down_kernel, tiles_per_seq=tiles_per_seq),
        grid=(m // tm, nk),
        in_specs=[pl.BlockSpec((tm, tk), lambda i, k: (i, k)),
                  pl.BlockSpec((HALO, tk), lambda i, k: (jnp.maximum(i * hb - 1, 0), k)),
                  pl.BlockSpec((HALO, tk), lambda i, k: (jnp.minimum((i + 1) * hb, n_halo - 1), k)),
                  pl.BlockSpec((tm, tk), lambda i, k: (i, nk + k)),
                  pl.BlockSpec((1, FFN_CONV, tk), lambda i, k: (k, 0, 0)),
                  pl.BlockSpec((1, 1, tk), lambda i, k: (k, 0, 0)),
                  pl.BlockSpec((tk, D_MODEL), lambda i, k: (k, 0)),
                  pl.BlockSpec((tm, D_MODEL), lambda i, k: (i, 0)),
                  pl.BlockSpec((1, 6, D_MODEL), lambda i, k: (i // tiles_per_seq, 0, 0)),
                  pl.BlockSpec((1, D_MODEL), lambda i, k: (0, 0))],
        out_specs=pl.BlockSpec((tm, D_MODEL), lambda i, k: (i, 0)),
        out_shape=jax.ShapeDtypeStruct((m, D_MODEL), F32),
        scratch_shapes=[pltpu.VMEM((tm, D_MODEL), F32)],
        compiler_params=_cparams(("parallel", "arbitrary")),
        name="ffn_down",
    )(up2d, up2d, up2d, up2d, cw, cb, w_down, x1_2d, mod, norm_w.reshape(1, -1))


def _rotary_tables(s):
    pos = jnp.arange(s, dtype=F32)
    inv_freq = ROPE_THETA ** (-jnp.arange(0, HEAD_DIM, 2, dtype=F32) / HEAD_DIM)
    ang = pos[:, None] * inv_freq[None, :]
    cos, sin = jnp.cos(ang), jnp.sin(ang)
    return jnp.concatenate([cos, cos], axis=-1), jnp.concatenate([-sin, sin], axis=-1)


def _prep_weights(w_in, w_attn_proj, w_ssd_proj, w_out, w_up, w_down):
    o = np.cumsum((0, ATTN_WIDTH, ATTN_WIDTH, ATTN_WIDTH, SSD_INNER, SSD_INNER + SSD_BC, 2 * SSD_HEADS,
                   D_MODEL, D_MODEL))
    wq, wk, wv = (w_in[:, o[t]:o[t + 1]].reshape(D_MODEL, N_ATTN_HEADS, HEAD_DIM) for t in range(3))
    w_qkv = jnp.stack([wq, wk, wv], axis=2).reshape(D_MODEL, QKV_WIDTH)
    w_rest = jnp.concatenate([w_in[:, o[3]:o[5]], w_in[:, o[6]:o[8]]], axis=1)
    w_dt = w_in[:, o[5]:o[6]]
    cast = lambda w: w.astype(BF16)
    return (cast(w_qkv), cast(w_rest), cast(w_dt), cast(w_attn_proj), cast(w_ssd_proj), cast(w_out),
            cast(w_up), cast(w_down))


def _trunk(x, mod, wts, small):
    bsz, s, _ = x.shape
    m = bsz * s
    w_qkv, w_rest, w_dt, w_attn_proj, w_ssd_proj, w_out, w_up, w_down = wts
    (norm_mix_w, ssd_conv_w, ssd_conv_b, dt_bias, a_log, d_skip, ssd_norm_w, norm_ffn_w, ffn_conv_w, ffn_conv_b,
     norm_f_w) = small
    h = _norm_mod(x, mod, norm_mix_w)
    h2d = h.reshape(m, D_MODEL)
    cos_t, sin_t = _rotary_tables(s)
    rest2d, dt_raw = _rest_proj(h2d, w_rest, w_dt)
    outs, lses = [], []
    for gi, (_, dilation) in enumerate(ATTN_GROUPS):
        o, l = _attention_group(_qkv_proj(h, w_qkv, cos_t, sin_t, gi, dilation), dilation)
        outs.append(o.reshape(m, ATTN_OUT_WIDTH))
        lses.append(l.reshape(m, ATTN_OUT_WIDTH))
    y = _ssd(rest2d.reshape(bsz, s, REST_WIDTH), dt_raw.reshape(bsz, s, 2 * SSD_HEADS), ssd_conv_w, ssd_conv_b,
             dt_bias, a_log, d_skip, ssd_norm_w)
    merged = _mix(_attn_merge(outs, lses), y.reshape(m, SSD_INNER), rest2d, w_attn_proj, w_ssd_proj)
    x1, h2 = _out_proj(merged.reshape(bsz, s, D_MODEL), w_out, x, mod, norm_ffn_w)
    up = _up_proj(h2.reshape(m, D_MODEL), w_up)
    out = _down_proj(up, ffn_conv_w, ffn_conv_b, w_down, x1.reshape(m, D_MODEL), mod, norm_f_w, s)
    return out.reshape(bsz, s, D_MODEL)


def kernel(x_prompt, x_sample, c_prompt, c_sample, w_ada, b_ada, norm_mix_w, w_in, ssd_conv_w, ssd_conv_b,
           dt_bias_fwd, dt_bias_bwd, a_log_fwd, a_log_bwd, ssd_d, ssd_norm_w, w_attn_proj, w_ssd_proj, w_out,
           norm_ffn_w, w_up, ffn_conv_w, ffn_conv_b, w_down, norm_f_w):
    wts = _prep_weights(w_in[0], w_attn_proj[0], w_ssd_proj[0], w_out[0], w_up[0], w_down[0])
    small = (norm_mix_w[0], ssd_conv_w[0], ssd_conv_b[0].reshape(1, -1),
             jnp.concatenate([dt_bias_fwd[0], dt_bias_bwd[0]]).reshape(1, -1),
             jnp.concatenate([a_log_fwd[0], a_log_bwd[0]]).reshape(1, -1),
             ssd_d[0], ssd_norm_w[0].reshape(1, -1),
             norm_ffn_w[0], ffn_conv_w[0], ffn_conv_b[0], norm_f_w)
    nb_p = c_prompt.shape[0]
    nb_s = c_sample.shape[0]
    pad = (-(nb_p + nb_s)) % 8
    c_all = jnp.concatenate([c_prompt, c_sample, jnp.zeros((pad, D_MODEL), F32)], axis=0)
    mod_all = _modulation(c_all, w_ada[0], b_ada[0]).reshape(-1, 6, D_MODEL)
    y_prompt = _trunk(x_prompt, mod_all[:nb_p], wts, small)
    y_sample = _trunk(x_sample, mod_all[nb_p:nb_p + nb_s], wts, small)
    return (y_prompt, y_sample)
```

```python
import functools

import numpy as np
import jax
import jax.numpy as jnp
from jax import lax
from jax.experimental import pallas as pl
from jax.experimental.pallas import tpu as pltpu

F32 = jnp.float32
BF16 = jnp.bfloat16

D_MODEL = 2048
HEAD_DIM = 128
ATTN_GROUPS = ((128, 1), (512, 4), (2048, 16))
HEADS_PER_GROUP = 4
N_ATTN_HEADS = HEADS_PER_GROUP * len(ATTN_GROUPS)
ATTN_WIDTH = N_ATTN_HEADS * HEAD_DIM
ATTN_OUT_WIDTH = HEADS_PER_GROUP * HEAD_DIM
ROPE_THETA = 10000.0
SSD_INNER = 2 * D_MODEL
SSD_HEAD_DIM = 64
SSD_HEADS = SSD_INNER // SSD_HEAD_DIM
SSD_GROUPS = 8
SSD_HEADS_PER_GROUP = SSD_HEADS // SSD_GROUPS
SSD_STATE = 128
SSD_CONV = 5
SSD_CHUNK = 128
SSD_BC = 2 * SSD_GROUPS * SSD_STATE
D_FF = 5632
FFN_CONV = 3
EPS = 1e-6
NEG_INF = -1e30
LOG2E = 1.4426950408889634

QKV_WIDTH = 3 * ATTN_WIDTH
REST_WIDTH = 2 * SSD_INNER + SSD_BC + 2 * D_MODEL
REST_Z, REST_XS, REST_BC, REST_GA, REST_GS = 0, SSD_INNER, 2 * SSD_INNER, 2 * SSD_INNER + SSD_BC, 2 * SSD_INNER + SSD_BC + D_MODEL

LANES = 128
SUBLANES = 8
HALO = 16
VMEM_LIMIT = 56 * 1024 * 1024
TM = 1024


def _cparams(sem):
    return pltpu.CompilerParams(dimension_semantics=sem, vmem_limit_bytes=VMEM_LIMIT)


def _silu(x):
    h = 0.5 * x
    return h + h * jnp.tanh(h)


def _shifted_rows(ext, shift, rows):
    if shift == 0:
        return ext[SUBLANES:SUBLANES + rows]
    return pltpu.roll(ext, (-shift) % ext.shape[0], 0)[SUBLANES:SUBLANES + rows]


def _with_halo(main_ref_val, prev_blk, next_blk, keep_prev, keep_next):
    prev = prev_blk.astype(F32)[HALO - SUBLANES:HALO] * keep_prev
    nxt = next_blk.astype(F32)[0:SUBLANES] * keep_next
    return jnp.concatenate([prev, main_ref_val.astype(F32), nxt], axis=0)


def _mod_kernel(c_ref, w_ref, b_ref, o_ref):
    c = c_ref[...]
    o_ref[...] = jnp.dot(c * jax.nn.sigmoid(c), w_ref[...], preferred_element_type=F32,
                         precision=lax.Precision.HIGHEST) + b_ref[...]


def _modulation(c_all, w_ada, b_ada):
    rows = c_all.shape[0]
    tn = 1024
    return pl.pallas_call(
        _mod_kernel,
        grid=(6 * D_MODEL // tn,),
        in_specs=[pl.BlockSpec((rows, D_MODEL), lambda j: (0, 0)),
                  pl.BlockSpec((D_MODEL, tn), lambda j: (0, j)),
                  pl.BlockSpec((1, tn), lambda j: (0, j))],
        out_specs=pl.BlockSpec((rows, tn), lambda j: (0, j)),
        out_shape=jax.ShapeDtypeStruct((rows, 6 * D_MODEL), F32),
        compiler_params=_cparams(("arbitrary",)),
        name="mod",
    )(c_all, w_ada, b_ada.reshape(1, -1))


def _rms_mod(x, w, scale, shift):
    y = x * lax.rsqrt(jnp.mean(x * x, axis=-1, keepdims=True) + EPS) * w
    return y * (1.0 + scale) + shift


def _norm_kernel(x_ref, mod_ref, w_ref, h_ref):
    h_ref[0] = _rms_mod(x_ref[0], w_ref[...], mod_ref[0, 1:2, :], mod_ref[0, 0:1, :]).astype(BF16)


def _norm_mod(x, mod, w):
    bsz, s, _ = x.shape
    ts = 512
    return pl.pallas_call(
        _norm_kernel,
        grid=(bsz, s // ts),
        in_specs=[pl.BlockSpec((1, ts, D_MODEL), lambda b, i: (b, i, 0)),
                  pl.BlockSpec((1, 6, D_MODEL), lambda b, i: (b, 0, 0)),
                  pl.BlockSpec((1, D_MODEL), lambda b, i: (0, 0))],
        out_specs=pl.BlockSpec((1, ts, D_MODEL), lambda b, i: (b, i, 0)),
        out_shape=jax.ShapeDtypeStruct((bsz, s, D_MODEL), BF16),
        compiler_params=_cparams(("parallel", "parallel")),
        name="norm_mix",
    )(x, mod, w.reshape(1, -1))


QKV_GROUP_WIDTH = 3 * ATTN_OUT_WIDTH
QKV_TN = QKV_GROUP_WIDTH


def _qkv_kernel(h_ref, w_ref, cos_ref, sin_ref, o_ref, *scratch, dilation):
    acc = jnp.dot(h_ref[0], w_ref[...], preferred_element_type=F32)
    cos = cos_ref[...]
    sin = sin_ref[...]
    tiles = QKV_TN // HEAD_DIM
    for t in range(tiles):
        sl = slice(t * HEAD_DIM, (t + 1) * HEAD_DIM)
        a = acc[:, sl]
        if t % 3 != 2:
            a = a * cos + pltpu.roll(a, HEAD_DIM // 2, 1) * sin
        if dilation == 1:
            o_ref[0, 0, :, sl] = a.astype(BF16)
        else:
            scratch[0][t] = a
    if dilation > 1:
        rows = TM // dilation
        for r in range(dilation):
            for t in range(tiles):
                sl = slice(t * HEAD_DIM, (t + 1) * HEAD_DIM)
                o_ref[0, r, :, sl] = scratch[0][t, pl.ds(r, rows, stride=dilation), :].astype(BF16)


def _qkv_proj(h, w_qkv, cos_t, sin_t, gi, dilation):
    bsz, s, _ = h.shape
    col_tiles = QKV_GROUP_WIDTH // QKV_TN
    rows = TM // dilation
    return pl.pallas_call(
        functools.partial(_qkv_kernel, dilation=dilation),
        grid=(bsz, s // TM, col_tiles),
        in_specs=[pl.BlockSpec((1, TM, D_MODEL), lambda b, i, j: (b, i, 0)),
                  pl.BlockSpec((D_MODEL, QKV_TN), lambda b, i, j: (0, gi * col_tiles + j)),
                  pl.BlockSpec((TM, HEAD_DIM), lambda b, i, j: (i, 0)),
                  pl.BlockSpec((TM, HEAD_DIM), lambda b, i, j: (i, 0))],
        out_specs=pl.BlockSpec((1, dilation, rows, QKV_TN), lambda b, i, j: (b, 0, i, j)),
        out_shape=jax.ShapeDtypeStruct((bsz, dilation, s // dilation, QKV_GROUP_WIDTH), BF16),
        scratch_shapes=[pltpu.VMEM((QKV_TN // HEAD_DIM, TM, HEAD_DIM), F32)] if dilation > 1 else [],
        compiler_params=_cparams(("parallel", "parallel", "arbitrary")),
        name=f"qkv_proj_d{dilation}",
    )(h, w_qkv, cos_t, sin_t)


REST_TN = 2048


def _rest_kernel(h_ref, w_ref, wdt_ref, o_ref, dt_ref):
    o_ref[...] = jnp.dot(h_ref[...], w_ref[...], preferred_element_type=F32).astype(BF16)

    @pl.when(pl.program_id(1) == 0)
    def _():
        dt_ref[...] = jnp.dot(h_ref[...], wdt_ref[...], preferred_element_type=F32)


def _rest_proj(h2d, w_rest, w_dt):
    m = h2d.shape[0]
    return pl.pallas_call(
        _rest_kernel,
        grid=(m // TM, REST_WIDTH // REST_TN),
        in_specs=[pl.BlockSpec((TM, D_MODEL), lambda i, j: (i, 0)),
                  pl.BlockSpec((D_MODEL, REST_TN), lambda i, j: (0, j)),
                  pl.BlockSpec((D_MODEL, 2 * SSD_HEADS), lambda i, j: (0, 0))],
        out_specs=[pl.BlockSpec((TM, REST_TN), lambda i, j: (i, j)),
                   pl.BlockSpec((TM, 2 * SSD_HEADS), lambda i, j: (i, 0))],
        out_shape=[jax.ShapeDtypeStruct((m, REST_WIDTH), BF16),
                   jax.ShapeDtypeStruct((m, 2 * SSD_HEADS), F32)],
        compiler_params=_cparams(("parallel", "arbitrary")),
        name="rest_proj",
    )(h2d, w_rest, w_dt)


ATTN_TQ = 128
ATTN_RADIUS = 64
ATTN_UNROLL = 8


def _attn_kernel(qkv_ref, o_ref, l_ref, *, n, dilation):
    tq = ATTN_TQ
    nq = n // tq
    win = min(n, tq + 2 * ATTN_RADIUS)
    scale = HEAD_DIM ** -0.5

    def block(it):
        r = it // nq
        i0 = pl.multiple_of((it % nq) * tq, tq)
        start = pl.multiple_of(jnp.clip(i0 - ATTN_RADIUS, 0, n - win), ATTN_RADIUS)
        rows = i0 + lax.broadcasted_iota(jnp.int32, (tq, win), 0)
        cols = start + lax.broadcasted_iota(jnp.int32, (tq, win), 1)
        valid = jnp.abs(cols - rows) <= ATTN_RADIUS
        q = qkv_ref[0, r, pl.ds(i0, tq), 0:HEAD_DIM]
        k = qkv_ref[0, r, pl.ds(start, win), HEAD_DIM:2 * HEAD_DIM]
        v = qkv_ref[0, r, pl.ds(start, win), 2 * HEAD_DIM:3 * HEAD_DIM]
        sc = lax.dot_general(q, k, (((1,), (1,)), ((), ())), preferred_element_type=F32) * scale
        sc = jnp.where(valid, sc, NEG_INF)
        mx = jnp.max(sc, axis=-1, keepdims=True)
        p = jnp.exp(sc - mx)
        den = jnp.sum(p, axis=-1, keepdims=True)
        o = jnp.dot(p.astype(BF16), v, preferred_element_type=F32) / den
        lse = jnp.broadcast_to(mx + jnp.log(den), (tq, HEAD_DIM))
        if dilation == 1:
            dst = pl.ds(i0, tq)
        else:
            dst = pl.ds(i0 * dilation + r, tq, stride=dilation)
        o_ref[0, dst, :] = o
        l_ref[0, dst, :] = lse

    def body(trip, carry):
        for u in range(ATTN_UNROLL):
            block(trip * ATTN_UNROLL + u)
        return carry

    lax.fori_loop(0, dilation * nq // ATTN_UNROLL, body, 0)


def _attention_group(qkv_g, dilation):
    bsz, _, n, _ = qkv_g.shape
    s = n * dilation
    assert (dilation * (n // ATTN_TQ)) % ATTN_UNROLL == 0
    ospec = pl.BlockSpec((1, s, HEAD_DIM), lambda b, hh: (b, 0, hh))
    return pl.pallas_call(
        functools.partial(_attn_kernel, n=n, dilation=dilation),
        grid=(bsz, HEADS_PER_GROUP),
        in_specs=[pl.BlockSpec((1, dilation, n, 3 * HEAD_DIM), lambda b, hh: (b, 0, 0, hh))],
        out_specs=[ospec, ospec],
        out_shape=[jax.ShapeDtypeStruct((bsz, s, ATTN_OUT_WIDTH), F32)] * 2,
        compiler_params=_cparams(("parallel", "parallel")),
        name=f"attn_d{dilation}",
    )(qkv_g)


SSD_CHUNKS_PER_STEP = 2


def _ssd_decay_terms(dt_raw, dtb_ref, alog_ref):
    L = SSD_CHUNK
    dt = jax.nn.softplus(dt_raw + dtb_ref[...])
    dta = dt * (-jnp.exp(alog_ref[...]))
    row = lax.broadcasted_iota(jnp.int32, (L, L), 0)
    col = lax.broadcasted_iota(jnp.int32, (L, L), 1)
    cum_f = jnp.dot((col <= row).astype(F32), dta, preferred_element_type=F32, precision=lax.Precision.HIGHEST)
    cum_b = jnp.dot((col >= row).astype(F32), dta, preferred_element_type=F32, precision=lax.Precision.HIGHEST)
    lane = lax.broadcasted_iota(jnp.int32, (L, 2 * SSD_HEADS), 1)
    acum_t = jnp.where(lane < SSD_HEADS, cum_f, cum_b).T
    dt_t = dt.T
    hrow = lax.broadcasted_iota(jnp.int32, (2 * SSD_HEADS, 1), 0)
    tot = jnp.where(hrow < SSD_HEADS, acum_t[:, L - 1:L], acum_t[:, 0:1])
    return acum_t, dt_t, tot


def _bcast_row(ref, r, rows):
    return jnp.broadcast_to(ref[pl.ds(r, 1), :], (rows, ref.shape[1]))


def _ssd_bwd_kernel(xs_ref, xsp_ref, xsn_ref, bc_ref, bcp_ref, bcn_ref, dt_ref, cwx_ref, cbx_ref, cwb_ref, cbb_ref,
                    dtb_ref, alog_ref, sout_ref, xt_out_ref, bc_out_ref, da_out_ref, state_ref, xt_ref, dec_ref, tot_ref):
    L, P, N, R = SSD_CHUNK, SSD_HEAD_DIM, SSD_STATE, SSD_HEADS_PER_GROUP
    gw = R * P
    c = pl.program_id(1)
    nb = pl.num_programs(1)
    cc = nb - 1 - c

    @pl.when(c == 0)
    def _():
        state_ref[...] = jnp.zeros_like(state_ref)

    keep_prev = jnp.where(cc > 0, 1.0, 0.0).astype(F32)
    keep_next = jnp.where(cc < nb - 1, 1.0, 0.0).astype(F32)

    def conv_silu(main_ref, prev_ref, next_ref, w_ref, b_ref, col0):
        sl = slice(col0, col0 + gw)
        ext = _with_halo(main_ref[0, :, sl], prev_ref[0, :, sl], next_ref[0, :, sl], keep_prev, keep_next)
        outs = []
        for sub in range(SSD_CHUNKS_PER_STEP):
            e = ext[sub * L:sub * L + L + 2 * SUBLANES]
            acc = b_ref[:, sl] + e[SUBLANES:SUBLANES + L] * w_ref[SSD_CONV // 2:SSD_CONV // 2 + 1, sl]
            for j in range(SSD_CONV):
                if j != SSD_CONV // 2:
                    acc = acc + _shifted_rows(e, j - SSD_CONV // 2, L) * w_ref[j:j + 1, sl]
            outs.append(_silu(acc))
        return outs

    for g in range(SSD_GROUPS):
        for sub, xg in enumerate(conv_silu(xs_ref, xsp_ref, xsn_ref, cwx_ref, cbx_ref, g * gw)):
            for q in range(gw // LANES):
                rows = slice(g * gw + q * LANES, g * gw + (q + 1) * LANES)
                x_t = xg[:, q * LANES:(q + 1) * LANES].T
                xt_ref[sub, rows, :] = x_t
                xt_out_ref[0, sub, rows, :] = x_t.astype(BF16)
    for q in range(SSD_BC // gw):
        for sub, bcg in enumerate(conv_silu(bc_ref, bcp_ref, bcn_ref, cwb_ref, cbb_ref, q * gw)):
            bc_out_ref[0, sub * L:(sub + 1) * L, q * gw:(q + 1) * gw] = bcg.astype(BF16)

    for sub in reversed(range(SSD_CHUNKS_PER_STEP)):
        trows = slice(sub * L, (sub + 1) * L)
        acum_t, dt_t, tot = _ssd_decay_terms(dt_ref[0, trows, :], dtb_ref, alog_ref)
        da_out_ref[0, sub, 0] = acum_t
        da_out_ref[0, sub, 1] = dt_t
        dec_ref[...] = jnp.exp(tot - acum_t) * dt_t
        tot_ref[...] = jnp.broadcast_to(jnp.exp(tot), (2 * SSD_HEADS, N))
        sout_ref[0, sub] = state_ref[...].astype(BF16)
        for g in range(SSD_GROUPS):
            b_g = bc_out_ref[0, trows, g * N:(g + 1) * N]
            for r in range(R):
                h = g * R + r
                hrows = slice(h * P, (h + 1) * P)
                hd = h + SSD_HEADS
                xsc = (xt_ref[sub, hrows, :] * _bcast_row(dec_ref, hd, P)).astype(BF16)
                state_ref[hrows, :] = (state_ref[hrows, :] * _bcast_row(tot_ref, hd, P)
                                       + jnp.dot(xsc, b_g, preferred_element_type=F32))


def _ssd_fwd_kernel(z_ref, xt_ref, bc_ref, da_ref, dskip_ref, nw_ref, sb_ref, y_ref,
                    state_ref, dec_ref, tot_ref, at_ref, ar_ref, dtt_ref, e_ref, acol_ref, ys_ref, yt_ref):
    L, P, N, R = SSD_CHUNK, SSD_HEAD_DIM, SSD_STATE, SSD_HEADS_PER_GROUP
    gw = R * P
    nt = (((1,), (1,)), ((), ()))

    @pl.when(pl.program_id(1) == 0)
    def _():
        state_ref[...] = jnp.zeros_like(state_ref)

    row = lax.broadcasted_iota(jnp.int32, (L, L), 0)
    col = lax.broadcasted_iota(jnp.int32, (L, L), 1)
    hrow = lax.broadcasted_iota(jnp.int32, (2 * SSD_HEADS, 1), 0)

    def build_acol(g):
        for r in range(R):
            hf = g * R + r
            u = jnp.where(row <= col, _bcast_row(at_ref, hf, L), _bcast_row(at_ref, hf + SSD_HEADS, L))
            acol_ref[hf] = u.T

    for sub in range(SSD_CHUNKS_PER_STEP):
        trows = slice(sub * L, (sub + 1) * L)
        acum_t = da_ref[0, sub, 0]
        dt_t = da_ref[0, sub, 1]
        tot = jnp.where(hrow < SSD_HEADS, acum_t[:, L - 1:L], acum_t[:, 0:1])
        dec_ref[...] = jnp.exp(tot - acum_t) * dt_t
        tot_ref[...] = jnp.broadcast_to(jnp.exp(tot), (2 * SSD_HEADS, N))
        at_ref[...] = acum_t * LOG2E
        ar_ref[...] = (acum_t - jnp.log(dt_t)) * LOG2E
        dtt_ref[...] = dt_t
        e_ref[...] = jnp.exp(acum_t)
        for g in range(SSD_GROUPS):
            build_acol(g)
        for g in range(SSD_GROUPS):
            b_g = bc_ref[0, trows, g * N:(g + 1) * N]
            c_g = bc_ref[0, trows, (SSD_GROUPS + g) * N:(SSD_GROUPS + g + 1) * N]
            grows = slice(g * gw, (g + 1) * gw)
            cb = lax.dot_general(c_g, b_g, nt, preferred_element_type=F32)
            cb_diag = jnp.sum(jnp.where(row == col, cb, 0.0), axis=0, keepdims=True)
            ys_ref[0, grows, :] = lax.dot_general(state_ref[grows, :].astype(BF16), c_g, nt, preferred_element_type=F32)
            ys_ref[1, grows, :] = lax.dot_general(sb_ref[0, sub, grows, :], c_g, nt, preferred_element_type=F32)
            for r in range(R):
                h = g * R + r
                hf = h
                hb = h + SSD_HEADS
                hrows = slice(h * P, (h + 1) * P)
                x_t = xt_ref[0, sub, hrows, :]
                x_f = x_t.astype(F32)
                arg = acol_ref[h] - jnp.where(row >= col, _bcast_row(ar_ref, hf, L), _bcast_row(ar_ref, hb, L))
                w = (cb * jnp.exp2(arg)).astype(BF16)
                y_t = lax.dot_general(x_t, w, nt, preferred_element_type=F32)
                y_t = y_t + ys_ref[0, hrows, :] * _bcast_row(e_ref, hf, P)
                y_t = y_t + ys_ref[1, hrows, :] * _bcast_row(e_ref, hb, P)
                yt_ref[hrows, :] = y_t + x_f * (dskip_ref[h] + cb_diag * dtt_ref[hb:hb + 1, :])
                xsc = (x_f * _bcast_row(dec_ref, hf, P)).astype(BF16)
                state_ref[hrows, :] = (state_ref[hrows, :] * _bcast_row(tot_ref, hf, P)
                                       + jnp.dot(xsc, b_g, preferred_element_type=F32))
            ys = []
            ssq = None
            for q in range(gw // LANES):
                sl = slice(g * gw + q * LANES, g * gw + (q + 1) * LANES)
                yb = yt_ref[sl, :].T * _silu(z_ref[0, trows, sl].astype(F32))
                ys.append(yb)
                part = jnp.sum(yb * yb, axis=-1, keepdims=True)
                ssq = part if ssq is None else ssq + part
            inv = lax.rsqrt(ssq * (1.0 / gw) + EPS)
            for q, yb in enumerate(ys):
                sl = slice(g * gw + q * LANES, g * gw + (q + 1) * LANES)
                y_ref[0, trows, sl] = (yb * inv * nw_ref[:, sl]).astype(BF16)


def _ssd(rest, dt_raw, conv_w, conv_b, dt_bias, a_log, d_skip, norm_w):
    bsz, s, _ = rest.shape
    L = SSD_CHUNK
    cps = SSD_CHUNKS_PER_STEP
    lb = cps * L
    nc = s // L
    nb = s // lb
    hb = lb // HALO
    n_halo = s // HALO
    cwx, cwb = conv_w[:, :SSD_INNER], conv_w[:, SSD_INNER:]
    cbx, cbb = conv_b[:, :SSD_INNER], conv_b[:, SSD_INNER:]

    rev = lambda c: nb - 1 - c
    prev = lambda col: (lambda b, c: (b, jnp.maximum(rev(c) * hb - 1, 0), col))
    nxt = lambda col: (lambda b, c: (b, jnp.minimum((rev(c) + 1) * hb, n_halo - 1), col))
    const = lambda b, c: (0, 0)
    xs_col, bc_col = REST_XS // SSD_INNER, REST_BC // SSD_BC
    state_scratch = [
        pltpu.VMEM((SSD_INNER, SSD_STATE), F32),
    ]
    decay_scratch = [
        pltpu.VMEM((2 * SSD_HEADS, L), F32),
        pltpu.VMEM((2 * SSD_HEADS, SSD_STATE), F32),
    ]

    states_b, xt, bc, decays = pl.pallas_call(
        _ssd_bwd_kernel,
        grid=(bsz, nb),
        in_specs=[
            pl.BlockSpec((1, lb, SSD_INNER), lambda b, c: (b, rev(c), xs_col)),
            pl.BlockSpec((1, HALO, SSD_INNER), prev(xs_col)),
            pl.BlockSpec((1, HALO, SSD_INNER), nxt(xs_col)),
            pl.BlockSpec((1, lb, SSD_BC), lambda b, c: (b, rev(c), bc_col)),
            pl.BlockSpec((1, HALO, SSD_BC), prev(bc_col)),
            pl.BlockSpec((1, HALO, SSD_BC), nxt(bc_col)),
            pl.BlockSpec((1, lb, 2 * SSD_HEADS), lambda b, c: (b, rev(c), 0)),
            pl.BlockSpec((SSD_CONV, SSD_INNER), const),
            pl.BlockSpec((1, SSD_INNER), const),
            pl.BlockSpec((SSD_CONV, SSD_BC), const),
            pl.BlockSpec((1, SSD_BC), const),
            pl.BlockSpec((1, 2 * SSD_HEADS), const),
            pl.BlockSpec((1, 2 * SSD_HEADS), const),
        ],
        out_specs=[pl.BlockSpec((1, cps, SSD_INNER, SSD_STATE), lambda b, c: (b, rev(c), 0, 0)),
                   pl.BlockSpec((1, cps, SSD_INNER, L), lambda b, c: (b, rev(c), 0, 0)),
                   pl.BlockSpec((1, lb, SSD_BC), lambda b, c: (b, rev(c), 0)),
                   pl.BlockSpec((1, cps, 2, 2 * SSD_HEADS, L), lambda b, c: (b, rev(c), 0, 0, 0))],
        out_shape=[jax.ShapeDtypeStruct((bsz, nc, SSD_INNER, SSD_STATE), BF16),
                   jax.ShapeDtypeStruct((bsz, nc, SSD_INNER, L), BF16),
                   jax.ShapeDtypeStruct((bsz, s, SSD_BC), BF16),
                   jax.ShapeDtypeStruct((bsz, nc, 2, 2 * SSD_HEADS, L), F32)],
        scratch_shapes=state_scratch + [pltpu.VMEM((cps, SSD_INNER, L), F32)] + decay_scratch,
        compiler_params=_cparams(("parallel", "arbitrary")),
        name="ssd_bwd_states",
    )(rest, rest, rest, rest, rest, rest, dt_raw, cwx, cbx, cwb, cbb, dt_bias, a_log)

    chunk4 = lambda b, c: (b, c, 0, 0)
    y = pl.pallas_call(
        _ssd_fwd_kernel,
        grid=(bsz, nb),
        in_specs=[
            pl.BlockSpec((1, lb, SSD_INNER), lambda b, c: (b, c, REST_Z // SSD_INNER)),
            pl.BlockSpec((1, cps, SSD_INNER, L), chunk4),
            pl.BlockSpec((1, lb, SSD_BC), lambda b, c: (b, c, 0)),
            pl.BlockSpec((1, cps, 2, 2 * SSD_HEADS, L), lambda b, c: (b, c, 0, 0, 0)),
            pl.BlockSpec(memory_space=pltpu.SMEM),
            pl.BlockSpec((1, SSD_INNER), const),
            pl.BlockSpec((1, cps, SSD_INNER, SSD_STATE), chunk4),
        ],
        out_specs=pl.BlockSpec((1, lb, SSD_INNER), lambda b, c: (b, c, 0)),
        out_shape=jax.ShapeDtypeStruct((bsz, s, SSD_INNER), BF16),
        scratch_shapes=state_scratch + decay_scratch + [
            pltpu.VMEM((2 * SSD_HEADS, L), F32),
            pltpu.VMEM((2 * SSD_HEADS, L), F32),
            pltpu.VMEM((2 * SSD_HEADS, L), F32),
            pltpu.VMEM((2 * SSD_HEADS, L), F32),
            pltpu.VMEM((SSD_HEADS, L, L), F32),
            pltpu.VMEM((2, SSD_INNER, L), F32),
            pltpu.VMEM((SSD_INNER, L), F32),
        ],
        compiler_params=_cparams(("parallel", "arbitrary")),
        name="ssd_fwd",
    )(rest, xt, bc, decays, d_skip, norm_w, states_b)
    return y


MIX_TN = 512


def _attn_merge_kernel(o0, o1, o2, l0, l1, l2, a_ref):
    ls = [l0[...], l1[...], l2[...]]
    mx = jnp.maximum(jnp.maximum(ls[0], ls[1]), ls[2])
    es = [jnp.exp(l - mx) for l in ls]
    den = es[0] + es[1] + es[2]
    num = es[0] * o0[...] + es[1] * o1[...] + es[2] * o2[...]
    a_ref[...] = (num / den).astype(BF16)


def _attn_merge(attn_outs, attn_lses):
    m = attn_outs[0].shape[0]
    tm = 512
    spec = pl.BlockSpec((tm, ATTN_OUT_WIDTH), lambda i: (i, 0))
    return pl.pallas_call(
        _attn_merge_kernel,
        grid=(m // tm,),
        in_specs=[spec] * 6,
        out_specs=spec,
        out_shape=jax.ShapeDtypeStruct((m, ATTN_OUT_WIDTH), BF16),
        compiler_params=_cparams(("parallel",)),
        name="attn_merge",
    )(*attn_outs, *attn_lses)


def _mix_kernel(a_ref, y_ref, ga_ref, gs_ref, wa_ref, ws_ref, m_ref):
    a_br = jnp.dot(a_ref[...], wa_ref[...], preferred_element_type=F32)
    s_br = jnp.dot(y_ref[...], ws_ref[...], preferred_element_type=F32)
    merged = jax.nn.sigmoid(ga_ref[...].astype(F32)) * a_br + jax.nn.sigmoid(gs_ref[...].astype(F32)) * s_br
    m_ref[...] = merged.astype(BF16)


def _mix(attn, y2d, rest2d, w_attn_proj, w_ssd_proj):
    m = y2d.shape[0]
    ga0 = REST_GA // MIX_TN
    gs0 = REST_GS // MIX_TN
    return pl.pallas_call(
        _mix_kernel,
        grid=(m // TM, D_MODEL // MIX_TN),
        in_specs=[
            pl.BlockSpec((TM, ATTN_OUT_WIDTH), lambda i, j: (i, 0)),
            pl.BlockSpec((TM, SSD_INNER), lambda i, j: (i, 0)),
            pl.BlockSpec((TM, MIX_TN), lambda i, j: (i, ga0 + j)),
            pl.BlockSpec((TM, MIX_TN), lambda i, j: (i, gs0 + j)),
            pl.BlockSpec((ATTN_OUT_WIDTH, MIX_TN), lambda i, j: (0, j)),
            pl.BlockSpec((SSD_INNER, MIX_TN), lambda i, j: (0, j)),
        ],
        out_specs=pl.BlockSpec((TM, MIX_TN), lambda i, j: (i, j)),
        out_shape=jax.ShapeDtypeStruct((m, D_MODEL), BF16),
        compiler_params=_cparams(("parallel", "arbitrary")),
        name="mix",
    )(attn, y2d, rest2d, rest2d, w_attn_proj, w_ssd_proj)


def _out_kernel(m_ref, w_ref, x_ref, mod_ref, nw_ref, x1_ref, h2_ref):
    mix = jnp.dot(m_ref[0], w_ref[...], preferred_element_type=F32)
    x1 = x_ref[0] + mod_ref[0, 2:3, :] * mix
    x1_ref[0] = x1
    h2_ref[0] = _rms_mod(x1, nw_ref[...], mod_ref[0, 4:5, :], mod_ref[0, 3:4, :]).astype(BF16)


def _out_proj(merged, w_out, x, mod, norm_w):
    bsz, s, _ = x.shape
    ts = 512
    tok = pl.BlockSpec((1, ts, D_MODEL), lambda b, i: (b, i, 0))
    return pl.pallas_call(
        _out_kernel,
        grid=(bsz, s // ts),
        in_specs=[tok,
                  pl.BlockSpec((D_MODEL, D_MODEL), lambda b, i: (0, 0)),
                  tok,
                  pl.BlockSpec((1, 6, D_MODEL), lambda b, i: (b, 0, 0)),
                  pl.BlockSpec((1, D_MODEL), lambda b, i: (0, 0))],
        out_specs=[tok, tok],
        out_shape=[jax.ShapeDtypeStruct((bsz, s, D_MODEL), F32),
                   jax.ShapeDtypeStruct((bsz, s, D_MODEL), BF16)],
        compiler_params=_cparams(("parallel", "parallel")),
        name="out_proj",
    )(merged, w_out, x, mod, norm_w.reshape(1, -1))


UP_TN = 2816


def _up_kernel(h_ref, w_ref, o_ref):
    o_ref[...] = jnp.dot(h_ref[...], w_ref[...], preferred_element_type=F32).astype(BF16)


def _up_proj(h2d, w_up):
    m = h2d.shape[0]
    return pl.pallas_call(
        _up_kernel,
        grid=(m // TM, 2 * D_FF // UP_TN),
        in_specs=[pl.BlockSpec((TM, D_MODEL), lambda i, j: (i, 0)),
                  pl.BlockSpec((D_MODEL, UP_TN), lambda i, j: (0, j))],
        out_specs=pl.BlockSpec((TM, UP_TN), lambda i, j: (i, j)),
        out_shape=jax.ShapeDtypeStruct((m, 2 * D_FF), BF16),
        compiler_params=_cparams(("parallel", "arbitrary")),
        name="ffn_up",
    )(h2d, w_up)


DOWN_TM = 512
DOWN_TK = 1408


def _down_kernel(g_ref, gp_ref, gn_ref, v_ref, cw_ref, cb_ref, w_ref, x1_ref, mod_ref, nw_ref, o_ref,
                 acc_ref, *, tiles_per_seq):
    i = pl.program_id(0)
    k = pl.program_id(1)
    tm = DOWN_TM

    @pl.when(k == 0)
    def _():
        acc_ref[...] = jnp.zeros_like(acc_ref)

    keep_prev = jnp.where(i % tiles_per_seq > 0, 1.0, 0.0).astype(F32)
    keep_next = jnp.where(i % tiles_per_seq < tiles_per_seq - 1, 1.0, 0.0).astype(F32)
    ext = _with_halo(g_ref[...], gp_ref[...], gn_ref[...], keep_prev, keep_next)
    gate = cb_ref[0] + ext[SUBLANES:SUBLANES + tm] * cw_ref[0, FFN_CONV // 2:FFN_CONV // 2 + 1, :]
    for j in range(FFN_CONV):
        if j != FFN_CONV // 2:
            gate = gate + _shifted_rows(ext, j - FFN_CONV // 2, tm) * cw_ref[0, j:j + 1, :]
    act = 0.5 * gate * (1.0 + lax.erf(gate * (2.0 ** -0.5))) * v_ref[...].astype(F32)
    acc_ref[...] += jnp.dot(act.astype(BF16), w_ref[...], preferred_element_type=F32)

    @pl.when(k == pl.num_programs(1) - 1)
    def _():
        x2 = x1_ref[...] + mod_ref[0, 5:6, :] * acc_ref[...]
        o_ref[...] = x2 * lax.rsqrt(jnp.mean(x2 * x2, axis=-1, keepdims=True) + EPS) * nw_ref[...]


def _down_proj(up2d, conv_w, conv_b, w_down, x1_2d, mod, norm_w, s):
    m = up2d.shape[0]
    tm, tk = DOWN_TM, DOWN_TK
    nk = D_FF // tk
    tiles_per_seq = s // tm
    hb = tm // HALO
    n_halo = m // HALO
    cw = conv_w.reshape(FFN_CONV, nk, tk).transpose(1, 0, 2)
    cb = conv_b.reshape(nk, 1, tk)
    return pl.pallas_call(
        functools.partial(_down_kernel, tiles_per_seq=tiles_per_seq),
        grid=(m // tm, nk),
        in_specs=[pl.BlockSpec((tm, tk), lambda i, k: (i, k)),
                  pl.BlockSpec((HALO, tk), lambda i, k: (jnp.maximum(i * hb - 1, 0), k)),
                  pl.BlockSpec((HALO, tk), lambda i, k: (jnp.minimum((i + 1) * hb, n_halo - 1), k)),
                  pl.BlockSpec((tm, tk), lambda i, k: (i, nk + k)),
                  pl.BlockSpec((1, FFN_CONV, tk), lambda i, k: (k, 0, 0)),
                  pl.BlockSpec((1, 1, tk), lambda i, k: (k, 0, 0)),
                  pl.BlockSpec((tk, D_MODEL), lambda i, k: (k, 0)),
                  pl.BlockSpec((tm, D_MODEL), lambda i, k: (i, 0)),
                  pl.BlockSpec((1, 6, D_MODEL), lambda i, k: (i // tiles_per_seq, 0, 0)),
                  pl.BlockSpec((1, D_MODEL), lambda i, k: (0, 0))],
        out_specs=pl.BlockSpec((tm, D_MODEL), lambda i, k: (i, 0)),
        out_shape=jax.ShapeDtypeStruct((m, D_MODEL), F32),
        scratch_shapes=[pltpu.VMEM((tm, D_MODEL), F32)],
        compiler_params=_cparams(("parallel", "arbitrary")),
        name="ffn_down",
    )(up2d, up2d, up2d, up2d, cw, cb, w_down, x1_2d, mod, norm_w.reshape(1, -1))


def _rotary_tables(s):
    pos = jnp.arange(s, dtype=F32)
    inv_freq = ROPE_THETA ** (-jnp.arange(0, HEAD_DIM, 2, dtype=F32) / HEAD_DIM)
    ang = pos[:, None] * inv_freq[None, :]
    cos, sin = jnp.cos(ang), jnp.sin(ang)
    return jnp.concatenate([cos, cos], axis=-1), jnp.concatenate([-sin, sin], axis=-1)


def _prep_weights(w_in, w_attn_proj, w_ssd_proj, w_out, w_up, w_down):
    o = np.cumsum((0, ATTN_WIDTH, ATTN_WIDTH, ATTN_WIDTH, SSD_INNER, SSD_INNER + SSD_BC, 2 * SSD_HEADS,
                   D_MODEL, D_MODEL))
    wq, wk, wv = (w_in[:, o[t]:o[t + 1]].reshape(D_MODEL, N_ATTN_HEADS, HEAD_DIM) for t in range(3))
    w_qkv = jnp.stack([wq, wk, wv], axis=2).reshape(D_MODEL, QKV_WIDTH)
    w_rest = jnp.concatenate([w_in[:, o[3]:o[5]], w_in[:, o[6]:o[8]]], axis=1)
    w_dt = w_in[:, o[5]:o[6]]
    cast = lambda w: w.astype(BF16)
    return (cast(w_qkv), cast(w_rest), cast(w_dt), cast(w_attn_proj), cast(w_ssd_proj), cast(w_out),
            cast(w_up), cast(w_down))


def _trunk(x, mod, wts, small):
    bsz, s, _ = x.shape
    m = bsz * s
    w_qkv, w_rest, w_dt, w_attn_proj, w_ssd_proj, w_out, w_up, w_down = wts
    (norm_mix_w, ssd_conv_w, ssd_conv_b, dt_bias, a_log, d_skip, ssd_norm_w, norm_ffn_w, ffn_conv_w, ffn_conv_b,
     norm_f_w) = small
    h = _norm_mod(x, mod, norm_mix_w)
    h2d = h.reshape(m, D_MODEL)
    cos_t, sin_t = _rotary_tables(s)
    rest2d, dt_raw = _rest_proj(h2d, w_rest, w_dt)
    outs, lses = [], []
    for gi, (_, dilation) in enumerate(ATTN_GROUPS):
        o, l = _attention_group(_qkv_proj(h, w_qkv, cos_t, sin_t, gi, dilation), dilation)
        outs.append(o.reshape(m, ATTN_OUT_WIDTH))
        lses.append(l.reshape(m, ATTN_OUT_WIDTH))
    y = _ssd(rest2d.reshape(bsz, s, REST_WIDTH), dt_raw.reshape(bsz, s, 2 * SSD_HEADS), ssd_conv_w, ssd_conv_b,
             dt_bias, a_log, d_skip, ssd_norm_w)
    merged = _mix(_attn_merge(outs, lses), y.reshape(m, SSD_INNER), rest2d, w_attn_proj, w_ssd_proj)
    x1, h2 = _out_proj(merged.reshape(bsz, s, D_MODEL), w_out, x, mod, norm_ffn_w)
    up = _up_proj(h2.reshape(m, D_MODEL), w_up)
    out = _down_proj(up, ffn_conv_w, ffn_conv_b, w_down, x1.reshape(m, D_MODEL), mod, norm_f_w, s)
    return out.reshape(bsz, s, D_MODEL)


def kernel(x_prompt, x_sample, c_prompt, c_sample, w_ada, b_ada, norm_mix_w, w_in, ssd_conv_w, ssd_conv_b,
           dt_bias_fwd, dt_bias_bwd, a_log_fwd, a_log_bwd, ssd_d, ssd_norm_w, w_attn_proj, w_ssd_proj, w_out,
           norm_ffn_w, w_up, ffn_conv_w, ffn_conv_b, w_down, norm_f_w):
    wts = _prep_weights(w_in[0], w_attn_proj[0], w_ssd_proj[0], w_out[0], w_up[0], w_down[0])
    small = (norm_mix_w[0], ssd_conv_w[0], ssd_conv_b[0].reshape(1, -1),
             jnp.concatenate([dt_bias_fwd[0], dt_bias_bwd[0]]).reshape(1, -1),
             jnp.concatenate([a_log_fwd[0], a_log_bwd[0]]).reshape(1, -1),
             ssd_d[0], ssd_norm_w[0].reshape(1, -1),
             norm_ffn_w[0], ffn_conv_w[0], ffn_conv_b[0], norm_f_w)
    nb_p = c_prompt.shape[0]
    nb_s = c_sample.shape[0]
    pad = (-(nb_p + nb_s)) % 8
    c_all = jnp.concatenate([c_prompt, c_sample, jnp.zeros((pad, D_MODEL), F32)], axis=0)
    mod_all = _modulation(c_all, w_ada[0], b_ada[0]).reshape(-1, 6, D_MODEL)
    y_prompt = _trunk(x_prompt, mod_all[:nb_p], wts, small)
    y_sample = _trunk(x_sample, mod_all[nb_p:nb_p + nb_s], wts, small)
    return (y_prompt, y_sample)
```
